```python
import jax, jax.numpy as jnp
from jax import lax
import numpy as np

D_MODEL = 1024
BATCH = 8
SEQ = 4096
DEPTH = 2
DEC_BATCH = 128
DEC_SEQ = 4
PAST_LEN = 16384
PAGE_SIZE = 128

N_MIXERS = 2
N_LRU_LAYERS = (DEPTH + 1) // 2
N_SWA_LAYERS = DEPTH // 2
D_RNN = D_MODEL
LRU_BLOCKS = 8
LRU_BLOCK = D_RNN // LRU_BLOCKS
CONV_W = 4
LRU_C = 8.0
N_HEADS = 16
HEAD_DIM = 64
N_KV_HEADS = 2
GROUP = N_HEADS // N_KV_HEADS
WINDOW = 128
BLK = WINDOW
QKV_DIM = (N_HEADS + 2 * N_KV_HEADS) * HEAD_DIM
D_FF = 4 * D_MODEL
ALPHA = (2.0 * DEPTH) ** 0.25
BETA = (8.0 * DEPTH) ** -0.25
LN_EPS = 1e-5

kernel_name = 'hybrid_rglru_swa_sink_decoder_step'


def layer_norm(x, g, b):
    xf = x.astype(jnp.float32)
    mu = jnp.mean(xf, axis=-1, keepdims=True)
    var = jnp.mean(jnp.square(xf - mu), axis=-1, keepdims=True)
    return ((xf - mu) * lax.rsqrt(var + LN_EPS) * g.astype(jnp.float32) + b.astype(jnp.float32)).astype(x.dtype)


def squared_relu_mlp(x, w_up, w_down):
    return jnp.square(jax.nn.relu(x @ w_up)) @ w_down


def recurrent_mixer(x, h0, conv_buf, w_x, b_x, w_y, b_y, conv_w, conv_b,
                    w_ga, b_ga, w_gi, b_gi, lam, w_out, b_out):
    B, T, _ = x.shape
    y = jax.nn.gelu(x @ w_y + b_y, approximate=True)
    u = x @ w_x + b_x
    upad = jnp.concatenate([conv_buf.astype(u.dtype), u], axis=1)
    xc = conv_b + sum(conv_w[k] * upad[:, k:k + T] for k in range(CONV_W))
    new_buf = upad[:, T:]
    xb = xc.reshape(B, T, LRU_BLOCKS, LRU_BLOCK)
    r = jax.nn.sigmoid(jnp.einsum('btni,nij->btnj', xb, w_ga) + b_ga).reshape(B, T, D_RNN)
    gi = jax.nn.sigmoid(jnp.einsum('btni,nij->btnj', xb, w_gi) + b_gi).reshape(B, T, D_RNN)
    log_a = -LRU_C * r.astype(jnp.float32) * jax.nn.softplus(-lam.astype(jnp.float32))
    a = jnp.exp(log_a)
    mult = jnp.sqrt(-jnp.expm1(2.0 * log_a))
    bterm = mult * (gi * xc).astype(jnp.float32)
    bterm = bterm.at[:, 0].add(a[:, 0] * h0.astype(jnp.float32))

    def combine(left, right):
        a1, b1 = left
        a2, b2 = right
        return a1 * a2, a2 * b1 + b2

    _, h = lax.associative_scan(combine, (a, bterm), axis=1)
    out = (h.astype(x.dtype) * y) @ w_out + b_out
    return out, h[:, -1].astype(x.dtype), new_buf


def split_qkv(x, w_qkv, b_qkv):
    B, T, _ = x.shape
    qkv = x @ w_qkv + b_qkv
    nq = N_HEADS * HEAD_DIM
    nk = N_KV_HEADS * HEAD_DIM
    q = qkv[..., :nq].reshape(B, T, N_KV_HEADS, GROUP, HEAD_DIM) * (HEAD_DIM ** -0.5)
    k = qkv[..., nq:nq + nk].reshape(B, T, N_KV_HEADS, HEAD_DIM)
    v = qkv[..., nq + nk:].reshape(B, T, N_KV_HEADS, HEAD_DIM)
    return q, k, v


def softmax_with_sink(scores, mask, sink):
    s = jnp.where(mask, scores, -jnp.inf)
    m = jnp.maximum(jnp.max(s, axis=-1, keepdims=True), sink)
    p = jnp.exp(s - m)
    return p / (jnp.sum(p, axis=-1, keepdims=True) + jnp.exp(sink - m))


def swa_prompt(x, w_qkv, b_qkv, sinks, w_o, b_o):
    B, S, _ = x.shape
    NB = S // BLK
    q, k, v = split_qkv(x, w_qkv, b_qkv)
    qb = q.reshape(B, NB, BLK, N_KV_HEADS, GROUP, HEAD_DIM)
    kb = k.reshape(B, NB, BLK, N_KV_HEADS, HEAD_DIM)
    vb = v.reshape(B, NB, BLK, N_KV_HEADS, HEAD_DIM)

    def band(t):
        prev = jnp.concatenate([jnp.zeros_like(t[:, :1]), t[:, :-1]], axis=1)
        return jnp.concatenate([prev, t], axis=2)

    kband, vband = band(kb), band(vb)
    scores = jnp.einsum('bnqkgd,bnskd->bnkgqs', qb, kband).astype(jnp.float32)
    blk0 = jnp.arange(NB)[:, None] * BLK
    qpos = blk0 + jnp.arange(BLK)[None, :]
    kpos = blk0 - BLK + jnp.arange(2 * BLK)[None, :]
    diff = qpos[:, :, None] - kpos[:, None, :]
    mask = (diff >= 0) & (diff < WINDOW) & (kpos[:, None, :] >= 0)
    sink = sinks.astype(jnp.float32).reshape(N_KV_HEADS, GROUP)[None, None, :, :, None, None]
    probs = softmax_with_sink(scores, mask[None, :, None, None], sink)
    o = jnp.einsum('bnkgqs,bnskd->bnqkgd', probs.astype(v.dtype), vband)
    out = o.reshape(B, S, N_HEADS * HEAD_DIM) @ w_o + b_o
    keep = min(WINDOW, S)
    return out, k[:, S - keep:], v[:, S - keep:]


def swa_sample(x, cache_k, cache_v, w_qkv, b_qkv, sinks, w_o, b_o):
    B, T, _ = x.shape
    w_buf = cache_k.shape[1]
    q, k, v = split_qkv(x, w_qkv, b_qkv)
    k_all = jnp.concatenate([cache_k.astype(k.dtype), k], axis=1)
    v_all = jnp.concatenate([cache_v.astype(v.dtype), v], axis=1)
    scores = jnp.einsum('bqkgd,bskd->bkgqs', q, k_all).astype(jnp.float32)
    qpos = PAST_LEN + jnp.arange(T)
    kpos = jnp.concatenate([PAST_LEN - w_buf + jnp.arange(w_buf), PAST_LEN + jnp.arange(T)])
    diff = qpos[:, None] - kpos[None, :]
    mask = (diff >= 0) & (diff < WINDOW)
    sink = sinks.astype(jnp.float32).reshape(N_KV_HEADS, GROUP)[None, :, :, None, None]
    probs = softmax_with_sink(scores, mask, sink)
    o = jnp.einsum('bkgqs,bskd->bqkgd', probs.astype(v.dtype), v_all)
    out = o.reshape(B, T, N_HEADS * HEAD_DIM) @ w_o + b_o
    return out, k_all[:, T:], v_all[:, T:]


def setup_inputs(seed: int = 0) -> dict:
    key = jax.random.key(seed)
    ks = iter(jax.random.split(key, 40))
    nrm = lambda shape, scale: jax.random.normal(next(ks), shape, jnp.float32) * scale
    w_buf = min(WINDOW, PAST_LEN)
    a0 = jax.random.uniform(next(ks), (N_LRU_LAYERS, D_RNN), jnp.float32, minval=0.9, maxval=0.999)
    return {
        'x_prompt': nrm((BATCH, SEQ, D_MODEL), 1.0),
        'x_sample': nrm((DEC_BATCH, DEC_SEQ, D_MODEL), 1.0),
        'state_lru_h': nrm((N_LRU_LAYERS, DEC_BATCH, D_RNN), 0.5),
        'state_lru_conv': nrm((N_LRU_LAYERS, DEC_BATCH, CONV_W - 1, D_RNN), 1.0),
        'cache_swa_k': nrm((N_SWA_LAYERS, DEC_BATCH, w_buf, N_KV_HEADS, HEAD_DIM), 1.0),
        'cache_swa_v': nrm((N_SWA_LAYERS, DEC_BATCH, w_buf, N_KV_HEADS, HEAD_DIM), 1.0),
        'lru_w_x': nrm((N_LRU_LAYERS, D_MODEL, D_RNN), D_MODEL ** -0.5),
        'lru_b_x': nrm((N_LRU_LAYERS, D_RNN), 0.01),
        'lru_w_y': nrm((N_LRU_LAYERS, D_MODEL, D_RNN), D_MODEL ** -0.5),
        'lru_b_y': nrm((N_LRU_LAYERS, D_RNN), 0.01),
        'lru_conv_w': nrm((N_LRU_LAYERS, CONV_W, D_RNN), CONV_W ** -0.5),
        'lru_conv_b': nrm((N_LRU_LAYERS, D_RNN), 0.01),
        'lru_w_ga': nrm((N_LRU_LAYERS, LRU_BLOCKS, LRU_BLOCK, LRU_BLOCK), LRU_BLOCK ** -0.5),
        'lru_b_ga': nrm((N_LRU_LAYERS, LRU_BLOCKS, LRU_BLOCK), 0.01),
        'lru_w_gi': nrm((N_LRU_LAYERS, LRU_BLOCKS, LRU_BLOCK, LRU_BLOCK), LRU_BLOCK ** -0.5),
        'lru_b_gi': nrm((N_LRU_LAYERS, LRU_BLOCKS, LRU_BLOCK), 0.01),
        'lru_lam': jnp.log(a0) - jnp.log1p(-a0),
        'lru_w_out': nrm((N_LRU_LAYERS, D_RNN, D_MODEL), BETA * D_RNN ** -0.5),
        'lru_b_out': nrm((N_LRU_LAYERS, D_MODEL), 0.01),
        'swa_w_qkv': nrm((N_SWA_LAYERS, D_MODEL, QKV_DIM), D_MODEL ** -0.5),
        'swa_b_qkv': nrm((N_SWA_LAYERS, QKV_DIM), 0.02),
        'swa_sinks': nrm((N_SWA_LAYERS, N_HEADS), 1.0),
        'swa_w_o': nrm((N_SWA_LAYERS, N_HEADS * HEAD_DIM, D_MODEL), BETA * (N_HEADS * HEAD_DIM) ** -0.5),
        'swa_b_o': nrm((N_SWA_LAYERS, D_MODEL), 0.01),
        'mlp_w_up': nrm((DEPTH, D_MODEL, D_FF), D_MODEL ** -0.5),
        'mlp_w_down': nrm((DEPTH, D_FF, D_MODEL), BETA * D_FF ** -0.5),
        'ln1_g': 1.0 + nrm((DEPTH, D_MODEL), 0.01),
        'ln1_b': nrm((DEPTH, D_MODEL), 0.01),
        'ln2_g': 1.0 + nrm((DEPTH, D_MODEL), 0.01),
        'ln2_b': nrm((DEPTH, D_MODEL), 0.01),
    }


def reference(x_prompt, x_sample, state_lru_h, state_lru_conv, cache_swa_k, cache_swa_v,
              lru_w_x, lru_b_x, lru_w_y, lru_b_y, lru_conv_w, lru_conv_b,
              lru_w_ga, lru_b_ga, lru_w_gi, lru_b_gi, lru_lam, lru_w_out, lru_b_out,
              swa_w_qkv, swa_b_qkv, swa_sinks, swa_w_o, swa_b_o,
              mlp_w_up, mlp_w_down, ln1_g, ln1_b, ln2_g, ln2_b):
    xp, xs = x_prompt, x_sample
    h_p, conv_p, k_p, v_p = [], [], [], []
    h_s, conv_s, k_s, v_s = [], [], [], []
    for i in range(DEPTH):
        j = i // N_MIXERS
        if i % N_MIXERS == 0:
            params = (lru_w_x[j], lru_b_x[j], lru_w_y[j], lru_b_y[j], lru_conv_w[j], lru_conv_b[j],
                      lru_w_ga[j], lru_b_ga[j], lru_w_gi[j], lru_b_gi[j], lru_lam[j],
                      lru_w_out[j], lru_b_out[j])
            h0 = jnp.zeros((xp.shape[0], D_RNN), xp.dtype)
            buf0 = jnp.zeros((xp.shape[0], CONV_W - 1, D_RNN), xp.dtype)
            mp, hp, cp = recurrent_mixer(xp, h0, buf0, *params)
            ms, hs, cs = recurrent_mixer(xs, state_lru_h[j], state_lru_conv[j], *params)
            h_p.append(hp); conv_p.append(cp); h_s.append(hs); conv_s.append(cs)
        else:
            params = (swa_w_qkv[j], swa_b_qkv[j], swa_sinks[j], swa_w_o[j], swa_b_o[j])
            mp, kp, vp = swa_prompt(xp, *params)
            ms, ks_, vs_ = swa_sample(xs, cache_swa_k[j], cache_swa_v[j], *params)
            k_p.append(kp); v_p.append(vp); k_s.append(ks_); v_s.append(vs_)
        xp = layer_norm(ALPHA * xp + mp, ln1_g[i], ln1_b[i])
        xs = layer_norm(ALPHA * xs + ms, ln1_g[i], ln1_b[i])
        xp = layer_norm(ALPHA * xp + squared_relu_mlp(xp, mlp_w_up[i], mlp_w_down[i]), ln2_g[i], ln2_b[i])
        xs = layer_norm(ALPHA * xs + squared_relu_mlp(xs, mlp_w_up[i], mlp_w_down[i]), ln2_g[i], ln2_b[i])
    return (xp, xs,
            jnp.stack(h_p), jnp.stack(conv_p), jnp.stack(k_p), jnp.stack(v_p),
            jnp.stack(h_s), jnp.stack(conv_s), jnp.stack(k_s), jnp.stack(v_s))
```

```python
import functools

import jax
import jax.numpy as jnp
from jax import lax
from jax.experimental import pallas as pl
from jax.experimental.pallas import tpu as pltpu

F32 = jnp.float32
BF16 = jnp.bfloat16

D_MODEL = 1024
D_RNN = 1024
D_FF = 4096
LRU_BLOCKS = 8
LRU_BLOCK = 128
CONV_W = 4
LRU_C = 8.0
N_HEADS = 16
HEAD_DIM = 64
N_KV = 2
GROUP = 8
WINDOW = 128
KV_DIM = N_KV * HEAD_DIM
Q_DIM = N_HEADS * HEAD_DIM
QKV_DIM = Q_DIM + 2 * KV_DIM
DEPTH = 2
ALPHA = (2.0 * DEPTH) ** 0.25
LN_EPS = 1e-5

SUBLANES = 8
VMEM_LIMIT = 48 * 1024 * 1024

MLP_TM = 512
MLP_FC = 1024
LRU_TT = 256
SWA_TT = 512
SMP_PAD_T = 8
SMP_BB = 16


def _const_spec(shape):
    nd = len(shape)
    return pl.BlockSpec(shape, lambda *_: (0,) * nd, pipeline_mode=pl.Buffered(1))


def _dot(a, b):
    return jnp.dot(a, b, preferred_element_type=F32)


def _dot_nt(a, b):
    return lax.dot_general(a, b, (((1,), (1,)), ((), ())), preferred_element_type=F32)


def _layer_norm(z, g, b):
    mu = jnp.mean(z, axis=-1, keepdims=True)
    d = z - mu
    var = jnp.mean(d * d, axis=-1, keepdims=True)
    return d * lax.rsqrt(var + LN_EPS) * g + b


def _sigmoid(x):
    return 0.5 * jnp.tanh(0.5 * x) + 0.5


def _mlp_kernel(x_ref, wup_ref, wdn_ref, g_ref, b_ref, o_ref):
    x = x_ref[...]
    xb = x.astype(BF16)
    acc = jnp.zeros(x.shape, F32)
    for j in range(D_FF // MLP_FC):
        h = _dot(xb, wup_ref[:, j * MLP_FC:(j + 1) * MLP_FC])
        h = jnp.square(jnp.maximum(h, 0.0)).astype(BF16)
        acc = acc + _dot(h, wdn_ref[j * MLP_FC:(j + 1) * MLP_FC, :])
    o_ref[...] = _layer_norm(ALPHA * x + acc, g_ref[...], b_ref[...])


def _mlp_block(x, w_up, w_dn, g, b):
    n = x.shape[0]
    tm = min(MLP_TM, n)
    assert n % tm == 0
    row_spec = pl.BlockSpec((tm, D_MODEL), lambda i: (i, 0))
    return pl.pallas_call(
        _mlp_kernel,
        grid=(n // tm,),
        in_specs=[row_spec, _const_spec((D_MODEL, D_FF)), _const_spec((D_FF, D_MODEL)),
                  _const_spec((1, D_MODEL)), _const_spec((1, D_MODEL))],
        out_specs=row_spec,
        out_shape=jax.ShapeDtypeStruct((n, D_MODEL), F32),
        compiler_params=pltpu.CompilerParams(
            dimension_semantics=("arbitrary",), vmem_limit_bytes=VMEM_LIMIT),
        name="mlp_ln",
    )(x, w_up, w_dn, g, b)


def _lru_in_proj(x, wxy_ref, bxy_ref):
    uy = _dot(x.astype(BF16), wxy_ref[...]) + bxy_ref[...]
    u = uy[:, :D_RNN]
    y = jax.nn.gelu(uy[:, D_RNN:], approximate=True)
    return u, y


def _neg_c_softplus_neg(lam):
    z = -lam
    sp = jnp.maximum(z, 0.0) + jnp.log1p(jnp.exp(-jnp.abs(z)))
    return -LRU_C * sp


def _lru_gate_chunk(xc_n, n, wg_ref, bg_ref, nla):
    lo = n * LRU_BLOCK
    gt = _dot(xc_n.astype(BF16), wg_ref[n]) + bg_ref[n]
    r = _sigmoid(gt[:, :LRU_BLOCK])
    gi = _sigmoid(gt[:, LRU_BLOCK:])
    log_a = r * nla[:, lo:lo + LRU_BLOCK]
    a = jnp.exp(log_a)
    mult = jnp.sqrt(-jnp.tanh(log_a) * (a * a + 1.0))
    return a, mult * (gi * xc_n)


def _lru_out(h, y, x, wo_ref, bo_ref, g_ref, b_ref):
    out = _dot((h * y).astype(BF16), wo_ref[...]) + bo_ref[...]
    return _layer_norm(ALPHA * x + out, g_ref[...], b_ref[...])


def _lru_prompt_kernel(x_ref, wxy_ref, bxy_ref, cw_ref, cb_ref, wg_ref, bg_ref, lam_ref,
                       wo_ref, bo_ref, g_ref, b_ref,
                       y_ref, hout_ref, cout_ref,
                       u_scr, a_scr, b_scr, h_scr, hc_scr):
    tt = LRU_TT
    t = pl.program_id(1)

    @pl.when(t == 0)
    def _():
        u_scr[0:SUBLANES, :] = jnp.zeros((SUBLANES, D_RNN), F32)
        hc_scr[...] = jnp.zeros(hc_scr.shape, F32)

    x = x_ref[...]
    u, y = _lru_in_proj(x, wxy_ref, bxy_ref)
    u_scr[SUBLANES:SUBLANES + tt, :] = u
    xc = cb_ref[...] + cw_ref[CONV_W - 1:CONV_W, :] * u
    for k in range(CONV_W - 1):
        shift = CONV_W - 1 - k
        xc = xc + cw_ref[k:k + 1, :] * u_scr[SUBLANES - shift:SUBLANES - shift + tt, :]

    nla = _neg_c_softplus_neg(lam_ref[...])
    for n in range(LRU_BLOCKS):
        lo = n * LRU_BLOCK
        a, b = _lru_gate_chunk(xc[:, lo:lo + LRU_BLOCK], n, wg_ref, bg_ref, nla)
        a_scr[:, lo:lo + LRU_BLOCK] = a
        b_scr[:, lo:lo + LRU_BLOCK] = b

    def step(i, h):
        h = a_scr[pl.ds(i, 1), :] * h + b_scr[pl.ds(i, 1), :]
        h_scr[pl.ds(i, 1), :] = h
        return h

    h_last = lax.fori_loop(0, tt, step, hc_scr[0:1, :], unroll=8)
    hc_scr[0:1, :] = h_last

    y_ref[...] = _lru_out(h_scr[...], y, x, wo_ref, bo_ref, g_ref, b_ref)
    u_scr[0:SUBLANES, :] = u_scr[tt:tt + SUBLANES, :]

    @pl.when(t == pl.num_programs(1) - 1)
    def _():
        hout_ref[...] = h_last
        cout_ref[...] = u_scr[tt + SUBLANES - (CONV_W - 1):tt + SUBLANES, :]


def _lru_sample_kernel(x_ref, h0_ref, cs_ref, wxy_ref, bxy_ref, cw_ref, cb_ref, wg_ref, bg_ref,
                       lam_ref, wo_ref, bo_ref, g_ref, b_ref,
                       y_ref, hout_ref, cout_ref, *, nb, nt):
    x = x_ref[...]
    u, y = _lru_in_proj(x, wxy_ref, bxy_ref)
    upad = jnp.concatenate([cs_ref[...], u], axis=0)
    xc = cb_ref[...]
    for k in range(CONV_W):
        xc = xc + cw_ref[k:k + 1, :] * upad[k * nb:(k + nt) * nb, :]
    nla = _neg_c_softplus_neg(lam_ref[...])
    a_parts, b_parts = [], []
    for n in range(LRU_BLOCKS):
        lo = n * LRU_BLOCK
        a, b = _lru_gate_chunk(xc[:, lo:lo + LRU_BLOCK], n, wg_ref, bg_ref, nla)
        a_parts.append(a)
        b_parts.append(b)
    a = jnp.concatenate(a_parts, axis=1)
    b = jnp.concatenate(b_parts, axis=1)
    h = h0_ref[...]
    hs = []
    for t in range(nt):
        h = a[t * nb:(t + 1) * nb, :] * h + b[t * nb:(t + 1) * nb, :]
        hs.append(h)
    y_ref[...] = _lru_out(jnp.concatenate(hs, axis=0), y, x, wo_ref, bo_ref, g_ref, b_ref)
    hout_ref[...] = h
    cout_ref[...] = upad[nt * nb:, :]


def _lru_weight_specs():
    return [_const_spec((D_MODEL, 2 * D_RNN)), _const_spec((1, 2 * D_RNN)),
            _const_spec((CONV_W, D_RNN)), _const_spec((1, D_RNN)),
            _const_spec((LRU_BLOCKS, LRU_BLOCK, 2 * LRU_BLOCK)), _const_spec((LRU_BLOCKS, 1, 2 * LRU_BLOCK)),
            _const_spec((1, D_RNN)), _const_spec((D_RNN, D_MODEL)), _const_spec((1, D_MODEL)),
            _const_spec((1, D_MODEL)), _const_spec((1, D_MODEL))]


def _lru_prompt(x, weights):
    bsz, seq, _ = x.shape
    tt = LRU_TT
    assert seq % tt == 0
    tile = pl.BlockSpec((None, tt, D_MODEL), lambda b, t: (b, t, 0))
    return pl.pallas_call(
        _lru_prompt_kernel,
        grid=(bsz, seq // tt),
        in_specs=[tile] + _lru_weight_specs(),
        out_specs=[tile,
                   pl.BlockSpec((None, 1, D_RNN), lambda b, t: (b, 0, 0)),
                   pl.BlockSpec((None, CONV_W - 1, D_RNN), lambda b, t: (b, 0, 0))],
        out_shape=[jax.ShapeDtypeStruct((bsz, seq, D_MODEL), F32),
                   jax.ShapeDtypeStruct((bsz, 1, D_RNN), F32),
                   jax.ShapeDtypeStruct((bsz, CONV_W - 1, D_RNN), F32)],
        scratch_shapes=[pltpu.VMEM((tt + SUBLANES, D_RNN), F32),
                        pltpu.VMEM((tt, D_RNN), F32),
                        pltpu.VMEM((tt, D_RNN), F32),
                        pltpu.VMEM((tt, D_RNN), F32),
                        pltpu.VMEM((SUBLANES, D_RNN), F32)],
        compiler_params=pltpu.CompilerParams(
            dimension_semantics=("arbitrary", "arbitrary"), vmem_limit_bytes=VMEM_LIMIT),
        name="lru_prompt",
    )(x, *weights)


def _lru_sample(x_tm, h0, cs_tm, weights, nb, nt):
    rows = nb * nt
    return pl.pallas_call(
        functools.partial(_lru_sample_kernel, nb=nb, nt=nt),
        grid=(1,),
        in_specs=[_const_spec((rows, D_MODEL)), _const_spec((nb, D_RNN)),
                  _const_spec(((CONV_W - 1) * nb, D_RNN))] + _lru_weight_specs(),
        out_specs=[_const_spec((rows, D_MODEL)), _const_spec((nb, D_RNN)),
                   _const_spec(((CONV_W - 1) * nb, D_RNN))],
        out_shape=[jax.ShapeDtypeStruct((rows, D_MODEL), F32),
                   jax.ShapeDtypeStruct((nb, D_RNN), F32),
                   jax.ShapeDtypeStruct(((CONV_W - 1) * nb, D_RNN), F32)],
        compiler_params=pltpu.CompilerParams(
            dimension_semantics=("arbitrary",), vmem_limit_bytes=VMEM_LIMIT),
        name="lru_sample",
    )(x_tm, h0, cs_tm, *weights)


def _lane_is_kv0(shape):
    return lax.broadcasted_iota(jnp.int32, shape, len(shape) - 1) < HEAD_DIM


def _sink_softmax_pv(s, mask, sink, vb):
    s = jnp.where(mask, s, -jnp.inf)
    m = jnp.maximum(jnp.max(s, axis=-1, keepdims=True), sink)
    p = jnp.exp(s - m)
    denom = jnp.sum(p, axis=-1, keepdims=True) + jnp.exp(sink - m)
    return _dot(p.astype(BF16), vb) / denom


def _swa_prompt_kernel(sink_ref, x_ref, wqkv_ref, bqkv_ref, wo_ref, bo_ref, g_ref, b_ref,
                       y_ref, kout_ref, vout_ref,
                       q_scr, k0_scr, k1_scr, v_scr, o_scr):
    tt = SWA_TT
    blk = WINDOW
    t = pl.program_id(1)

    @pl.when(t == 0)
    def _():
        zeros = jnp.zeros((blk, KV_DIM), BF16)
        k0_scr[0:blk, :] = zeros
        k1_scr[0:blk, :] = zeros
        v_scr[0:blk, :] = zeros

    x = x_ref[...]
    qkv = _dot(x.astype(BF16), wqkv_ref[...]) + bqkv_ref[...]
    k = qkv[:, Q_DIM:Q_DIM + KV_DIM]
    v = qkv[:, Q_DIM + KV_DIM:]
    q_scr[...] = (qkv[:, :Q_DIM] * (HEAD_DIM ** -0.5)).astype(BF16)
    kv0 = _lane_is_kv0((tt, KV_DIM))
    k0_scr[blk:, :] = jnp.where(kv0, k, 0.0).astype(BF16)
    k1_scr[blk:, :] = jnp.where(kv0, 0.0, k).astype(BF16)
    v_scr[blk:, :] = v.astype(BF16)

    qi = lax.broadcasted_iota(jnp.int32, (blk, 2 * blk), 0)
    kj = lax.broadcasted_iota(jnp.int32, (blk, 2 * blk), 1)
    band = (kj > qi) & (kj <= qi + blk)
    kv0_o = _lane_is_kv0((blk, KV_DIM))

    for i in range(tt // blk):
        first = jnp.logical_and(t == 0, i == 0)
        mask = band & (kj >= jnp.where(first, blk, 0))
        r0 = i * blk
        qs = jnp.concatenate(
            [q_scr[r0:r0 + blk, g * KV_DIM:(g + 1) * KV_DIM] for g in range(GROUP)], axis=0)
        vb = v_scr[r0:r0 + 2 * blk, :]
        outs = []
        for kh, k_scr in enumerate((k0_scr, k1_scr)):
            s = _dot_nt(qs, k_scr[r0:r0 + 2 * blk, :])
            outs.append([
                _sink_softmax_pv(s[g * blk:(g + 1) * blk, :], mask, sink_ref[kh * GROUP + g], vb)
                for g in range(GROUP)])
        for g in range(GROUP):
            o_scr[r0:r0 + blk, g * KV_DIM:(g + 1) * KV_DIM] = jnp.where(kv0_o, outs[0][g], outs[1][g])

    out = _dot(o_scr[...].astype(BF16), wo_ref[...]) + bo_ref[...]
    y_ref[...] = _layer_norm(ALPHA * x + out, g_ref[...], b_ref[...])

    k0_scr[0:blk, :] = k0_scr[tt:tt + blk, :]
    k1_scr[0:blk, :] = k1_scr[tt:tt + blk, :]
    v_scr[0:blk, :] = v_scr[tt:tt + blk, :]

    @pl.when(t == pl.num_programs(1) - 1)
    def _():
        kout_ref[...] = k[tt - blk:, :]
        vout_ref[...] = v[tt - blk:, :]


def _swa_weight_specs():
    return [_const_spec((D_MODEL, QKV_DIM)), _const_spec((1, QKV_DIM)),
            _const_spec((Q_DIM, D_MODEL)), _const_spec((1, D_MODEL)),
            _const_spec((1, D_MODEL)), _const_spec((1, D_MODEL))]


def _swa_prompt(x, sinks, weights):
    bsz, seq, _ = x.shape
    tt = SWA_TT
    assert seq % tt == 0 and seq >= WINDOW
    tile = pl.BlockSpec((None, tt, D_MODEL), lambda b, t: (b, t, 0))
    kv_spec = pl.BlockSpec((None, WINDOW, KV_DIM), lambda b, t: (b, 0, 0))
    return pl.pallas_call(
        _swa_prompt_kernel,
        grid=(bsz, seq // tt),
        in_specs=[pl.BlockSpec(memory_space=pltpu.SMEM), tile] + _swa_weight_specs(),
        out_specs=[tile, kv_spec, kv_spec],
        out_shape=[jax.ShapeDtypeStruct((bsz, seq, D_MODEL), F32),
                   jax.ShapeDtypeStruct((bsz, WINDOW, KV_DIM), F32),
                   jax.ShapeDtypeStruct((bsz, WINDOW, KV_DIM), F32)],
        scratch_shapes=[pltpu.VMEM((tt, Q_DIM), BF16),
                        pltpu.VMEM((tt + WINDOW, KV_DIM), BF16),
                        pltpu.VMEM((tt + WINDOW, KV_DIM), BF16),
                        pltpu.VMEM((tt + WINDOW, KV_DIM), BF16),
                        pltpu.VMEM((tt, Q_DIM), F32)],
        compiler_params=pltpu.CompilerParams(
            dimension_semantics=("arbitrary", "arbitrary"), vmem_limit_bytes=VMEM_LIMIT),
        name="swa_prompt",
    )(sinks, x, *weights)


def _swa_sample_kernel(sink_ref, x_ref, ck_ref, cv_ref, wqkv_ref, bqkv_ref, wo_ref, bo_ref, g_ref, b_ref,
                       y_ref, knew_ref, vnew_ref,
                       q_scr, kb_scr, vb_scr, o_scr, *, nt):
    pt = SMP_PAD_T
    wbuf = ck_ref.shape[1]
    nkeys = kb_scr.shape[0]
    step = pl.program_id(0)

    @pl.when(step == 0)
    def _():
        qkv = _dot(x_ref[...].astype(BF16), wqkv_ref[...]) + bqkv_ref[...]
        q_scr[...] = qkv[:, :Q_DIM] * (HEAD_DIM ** -0.5)
        knew_ref[...] = qkv[:, Q_DIM:Q_DIM + KV_DIM]
        vnew_ref[...] = qkv[:, Q_DIM + KV_DIM:]
        kb_scr[...] = jnp.zeros(kb_scr.shape, F32)
        vb_scr[...] = jnp.zeros(vb_scr.shape, F32)

    rows = GROUP * pt
    ri = lax.broadcasted_iota(jnp.int32, (rows, nkeys), 0)
    kj = lax.broadcasted_iota(jnp.int32, (rows, nkeys), 1)
    tq = ri % pt
    in_buf = kj < wbuf
    mask = (in_buf & (kj - wbuf + WINDOW > tq)) | (
        jnp.logical_not(in_buf) & (kj - wbuf <= tq) & (kj - wbuf < nt))
    gi = lax.broadcasted_iota(jnp.int32, (rows, 1), 0) // pt
    sink_cols = []
    for kh in range(N_KV):
        col = jnp.zeros((rows, 1), F32)
        for g in range(GROUP):
            col = jnp.where(gi == g, sink_ref[kh * GROUP + g], col)
        sink_cols.append(col)
    kv0_k = _lane_is_kv0((nkeys, KV_DIM))
    kv0_o = _lane_is_kv0((rows, KV_DIM))

    def per_batch(b, carry):
        r0 = pl.multiple_of((step * SMP_BB + b) * pt, pt)
        kb_scr[0:wbuf, :] = ck_ref[b]
        vb_scr[0:wbuf, :] = cv_ref[b]
        kb_scr[wbuf:wbuf + pt, :] = knew_ref[pl.ds(r0, pt), :]
        vb_scr[wbuf:wbuf + pt, :] = vnew_ref[pl.ds(r0, pt), :]
        qs = jnp.concatenate(
            [q_scr[pl.ds(r0, pt), g * KV_DIM:(g + 1) * KV_DIM] for g in range(GROUP)], axis=0)
        qs = qs.astype(BF16)
        kb = kb_scr[...]
        vb = vb_scr[...].astype(BF16)
        o0 = _sink_softmax_pv(_dot_nt(qs, jnp.where(kv0_k, kb, 0.0).astype(BF16)), mask, sink_cols[0], vb)
        o1 = _sink_softmax_pv(_dot_nt(qs, jnp.where(kv0_k, 0.0, kb).astype(BF16)), mask, sink_cols[1], vb)
        o = jnp.where(kv0_o, o0, o1)
        for g in range(GROUP):
            o_scr[pl.ds(r0, pt), g * KV_DIM:(g + 1) * KV_DIM] = o[g * pt:(g + 1) * pt, :]
        return carry

    lax.fori_loop(0, SMP_BB, per_batch, 0)

    @pl.when(step == pl.num_programs(0) - 1)
    def _():
        out = _dot(o_scr[...].astype(BF16), wo_ref[...]) + bo_ref[...]
        y_ref[...] = _layer_norm(ALPHA * x_ref[...] + out, g_ref[...], b_ref[...])


def _swa_sample(x_pad, cache_k, cache_v, sinks, weights, nt):
    rows = x_pad.shape[0]
    nb, wbuf, _ = cache_k.shape
    assert nb % SMP_BB == 0 and rows == nb * SMP_PAD_T and nt <= SMP_PAD_T
    nkeys = 2 * WINDOW
    assert wbuf + SMP_PAD_T <= nkeys
    cache_spec = pl.BlockSpec((SMP_BB, wbuf, KV_DIM), lambda i: (i, 0, 0))
    return pl.pallas_call(
        functools.partial(_swa_sample_kernel, nt=nt),
        grid=(nb // SMP_BB,),
        in_specs=[pl.BlockSpec(memory_space=pltpu.SMEM), _const_spec((rows, D_MODEL)),
                  cache_spec, cache_spec] + _swa_weight_specs(),
        out_specs=[_const_spec((rows, D_MODEL)), _const_spec((rows, KV_DIM)), _const_spec((rows, KV_DIM))],
        out_shape=[jax.ShapeDtypeStruct((rows, D_MODEL), F32),
                   jax.ShapeDtypeStruct((rows, KV_DIM), F32),
                   jax.ShapeDtypeStruct((rows, KV_DIM), F32)],
        scratch_shapes=[pltpu.VMEM((rows, Q_DIM), F32),
                        pltpu.VMEM((nkeys, KV_DIM), F32),
                        pltpu.VMEM((nkeys, KV_DIM), F32),
                        pltpu.VMEM((rows, Q_DIM), F32)],
        compiler_params=pltpu.CompilerParams(
            dimension_semantics=("arbitrary",), vmem_limit_bytes=VMEM_LIMIT),
        name="swa_sample",
    )(sinks, x_pad, cache_k, cache_v, *weights)


def _row(v):
    return v.reshape(1, -1).astype(F32)


def _pack_lru_weights(j, lru_w_x, lru_b_x, lru_w_y, lru_b_y, lru_conv_w, lru_conv_b,
                      lru_w_ga, lru_b_ga, lru_w_gi, lru_b_gi, lru_lam, lru_w_out, lru_b_out, g, b):
    wxy = jnp.concatenate([lru_w_x[j], lru_w_y[j]], axis=1).astype(BF16)
    bxy = jnp.concatenate([lru_b_x[j], lru_b_y[j]]).reshape(1, -1)
    wg = jnp.concatenate([lru_w_ga[j], lru_w_gi[j]], axis=2).astype(BF16)
    bg = jnp.concatenate([lru_b_ga[j], lru_b_gi[j]], axis=1)[:, None, :]
    return (wxy, bxy, lru_conv_w[j], _row(lru_conv_b[j]), wg, bg, _row(lru_lam[j]),
            lru_w_out[j].astype(BF16), _row(lru_b_out[j]), _row(g), _row(b))


def _pack_swa_weights(j, swa_w_qkv, swa_b_qkv, swa_w_o, swa_b_o, g, b):
    w, bias = swa_w_qkv[j], swa_b_qkv[j]
    wq = w[:, :Q_DIM].reshape(D_MODEL, N_KV, GROUP, HEAD_DIM).transpose(0, 2, 1, 3).reshape(D_MODEL, Q_DIM)
    bq = bias[:Q_DIM].reshape(N_KV, GROUP, HEAD_DIM).transpose(1, 0, 2).reshape(Q_DIM)
    wqkv = jnp.concatenate([wq, w[:, Q_DIM:]], axis=1).astype(BF16)
    bqkv = jnp.concatenate([bq, bias[Q_DIM:]]).reshape(1, -1)
    wo = swa_w_o[j].reshape(N_KV, GROUP, HEAD_DIM, D_MODEL).transpose(1, 0, 2, 3).reshape(Q_DIM, D_MODEL)
    return (wqkv, bqkv, wo.astype(BF16), _row(swa_b_o[j]), _row(g), _row(b))


def kernel(x_prompt, x_sample, state_lru_h, state_lru_conv, cache_swa_k, cache_swa_v,
           lru_w_x, lru_b_x, lru_w_y, lru_b_y, lru_conv_w, lru_conv_b,
           lru_w_ga, lru_b_ga, lru_w_gi, lru_b_gi, lru_lam, lru_w_out, lru_b_out,
           swa_w_qkv, swa_b_qkv, swa_sinks, swa_w_o, swa_b_o,
           mlp_w_up, mlp_w_down, ln1_g, ln1_b, ln2_g, ln2_b):
    bsz, seq, _ = x_prompt.shape
    nb, nt, _ = x_sample.shape
    wbuf = cache_swa_k.shape[2]

    def mlp(x2d, i):
        return _mlp_block(x2d, mlp_w_up[i].astype(BF16), mlp_w_down[i].astype(BF16),
                          _row(ln2_g[i]), _row(ln2_b[i]))

    lw = _pack_lru_weights(0, lru_w_x, lru_b_x, lru_w_y, lru_b_y, lru_conv_w, lru_conv_b,
                           lru_w_ga, lru_b_ga, lru_w_gi, lru_b_gi, lru_lam, lru_w_out, lru_b_out,
                           ln1_g[0], ln1_b[0])
    xp, h_p, conv_p = _lru_prompt(x_prompt, lw)
    xp = mlp(xp.reshape(bsz * seq, D_MODEL), 0)

    xs_tm = x_sample.transpose(1, 0, 2).reshape(nt * nb, D_MODEL)
    cs_tm = state_lru_conv[0].transpose(1, 0, 2).reshape((CONV_W - 1) * nb, D_RNN)
    xs_tm, h_s, conv_s_tm = _lru_sample(xs_tm, state_lru_h[0], cs_tm, lw, nb, nt)
    xs_tm = mlp(xs_tm, 0)
    conv_s = conv_s_tm.reshape(CONV_W - 1, nb, D_RNN).transpose(1, 0, 2)

    sw = _pack_swa_weights(0, swa_w_qkv, swa_b_qkv, swa_w_o, swa_b_o, ln1_g[1], ln1_b[1])
    sinks = swa_sinks[0].astype(F32)
    xp, k_p, v_p = _swa_prompt(xp.reshape(bsz, seq, D_MODEL), sinks, sw)
    xp = mlp(xp.reshape(bsz * seq, D_MODEL), 1).reshape(bsz, seq, D_MODEL)

    xs_bm = xs_tm.reshape(nt, nb, D_MODEL).transpose(1, 0, 2)
    xs_pad = jnp.pad(xs_bm, ((0, 0), (0, SMP_PAD_T - nt), (0, 0))).reshape(nb * SMP_PAD_T, D_MODEL)
    ck = cache_swa_k[0].reshape(nb, wbuf, KV_DIM)
    cv = cache_swa_v[0].reshape(nb, wbuf, KV_DIM)
    ys_pad, k_new, v_new = _swa_sample(xs_pad, ck, cv, sinks, sw, nt)
    xs = ys_pad.reshape(nb, SMP_PAD_T, D_MODEL)[:, :nt].reshape(nb * nt, D_MODEL)
    xs = mlp(xs, 1).reshape(nb, nt, D_MODEL)
    k_new = k_new.reshape(nb, SMP_PAD_T, KV_DIM)[:, :nt]
    v_new = v_new.reshape(nb, SMP_PAD_T, KV_DIM)[:, :nt]
    k_s = jnp.concatenate([ck, k_new], axis=1)[:, nt:]
    v_s = jnp.concatenate([cv, v_new], axis=1)[:, nt:]

    kv_shape = (1, -1, min(WINDOW, seq), N_KV, HEAD_DIM)
    return (xp, xs,
            h_p.reshape(1, bsz, D_RNN), conv_p[None],
            k_p.reshape(kv_shape), v_p.reshape(kv_shape),
            h_s[None], conv_s[None],
            k_s.reshape(1, nb, wbuf, N_KV, HEAD_DIM), v_s.reshape(1, nb, wbuf, N_KV, HEAD_DIM))
```

```python
import functools

import jax
import jax.numpy as jnp
from jax import lax
from jax.experimental import pallas as pl
from jax.experimental.pallas import tpu as pltpu

F32 = jnp.float32
BF16 = jnp.bfloat16

D_MODEL = 1024
D_RNN = 1024
D_FF = 4096
LRU_BLOCKS = 8
LRU_BLOCK = 128
CONV_W = 4
LRU_C = 8.0
N_HEADS = 16
HEAD_DIM = 64
N_KV = 2
GROUP = 8
WINDOW = 128
KV_DIM = N_KV * HEAD_DIM
Q_DIM = N_HEADS * HEAD_DIM
QKV_DIM = Q_DIM + 2 * KV_DIM
DEPTH = 2
ALPHA = (2.0 * DEPTH) ** 0.25
LN_EPS = 1e-5

SUBLANES = 8
VMEM_LIMIT = 48 * 1024 * 1024

MLP_TM = 512
MLP_FC = 1024
LRU_TT = 256
SWA_TT = 512
SMP_PAD_T = 8
SMP_BB = 16


def _const_spec(shape):
    nd = len(shape)
    return pl.BlockSpec(shape, lambda *_: (0,) * nd, pipeline_mode=pl.Buffered(1))


def _dot(a, b):
    return jnp.dot(a, b, preferred_element_type=F32)


def _dot_nt(a, b):
    return lax.dot_general(a, b, (((1,), (1,)), ((), ())), preferred_element_type=F32)


def _layer_norm(z, g, b):
    mu = jnp.mean(z, axis=-1, keepdims=True)
    d = z - mu
    var = jnp.mean(d * d, axis=-1, keepdims=True)
    return d * lax.rsqrt(var + LN_EPS) * g + b


def _sigmoid(x):
    return 0.5 * jnp.tanh(0.5 * x) + 0.5


def _mlp_ln(x, wup_ref, wdn_ref, g_ref, b_ref):
    xb = x.astype(BF16)
    acc = jnp.zeros(x.shape, F32)
    for j in range(D_FF // MLP_FC):
        h = _dot(xb, wup_ref[:, j * MLP_FC:(j + 1) * MLP_FC])
        h = jnp.square(jnp.maximum(h, 0.0)).astype(BF16)
        acc = acc + _dot(h, wdn_ref[j * MLP_FC:(j + 1) * MLP_FC, :])
    return _layer_norm(ALPHA * x + acc, g_ref[...], b_ref[...])


def _mlp_kernel(x_ref, wup_ref, wdn_ref, g_ref, b_ref, o_ref):
    o_ref[...] = _mlp_ln(x_ref[...], wup_ref, wdn_ref, g_ref, b_ref)


def _mlp_weight_specs():
    return [_const_spec((D_MODEL, D_FF)), _const_spec((D_FF, D_MODEL)),
            _const_spec((1, D_MODEL)), _const_spec((1, D_MODEL))]


def _mlp_block(x, w_up, w_dn, g, b):
    n = x.shape[0]
    tm = min(MLP_TM, n)
    assert n % tm == 0
    row_spec = pl.BlockSpec((tm, D_MODEL), lambda i: (i, 0))
    return pl.pallas_call(
        _mlp_kernel,
        grid=(n // tm,),
        in_specs=[row_spec] + _mlp_weight_specs(),
        out_specs=row_spec,
        out_shape=jax.ShapeDtypeStruct((n, D_MODEL), F32),
        compiler_params=pltpu.CompilerParams(
            dimension_semantics=("arbitrary",), vmem_limit_bytes=VMEM_LIMIT),
        name="mlp_ln",
    )(x, w_up, w_dn, g, b)


_GELU_C0 = (2.0 / jnp.pi) ** 0.5
_GELU_C1 = _GELU_C0 * 0.044715


def _gelu_tanh(x):
    t = jnp.tanh(x * (_GELU_C0 + _GELU_C1 * (x * x)))
    hx = 0.5 * x
    return hx + hx * t


def _lru_in_proj(x, wxy_ref, bxy_ref):
    uy = _dot(x.astype(BF16), wxy_ref[...]) + bxy_ref[...]
    u = uy[:, :D_RNN]
    y = _gelu_tanh(uy[:, D_RNN:])
    return u, y


def _neg_c_softplus_neg(lam):
    z = -lam
    sp = jnp.maximum(z, 0.0) + jnp.log1p(jnp.exp(-jnp.abs(z)))
    return -LRU_C * sp


def _lru_gate_chunk(xc_n, n, wg_ref, bg_ref, nla):
    return _lru_gate_math(_dot(xc_n.astype(BF16), wg_ref[n]), xc_n, n, bg_ref, nla)


def _lru_gate_math(gt, xc_n, n, bg_ref, nla):
    lo = n * LRU_BLOCK
    gt = gt + bg_ref[n]
    r = _sigmoid(gt[:, :LRU_BLOCK])
    gi = _sigmoid(gt[:, LRU_BLOCK:])
    log_a = r * nla[:, lo:lo + LRU_BLOCK]
    a = jnp.exp(log_a)
    v = -jnp.tanh(log_a) * (a * a + 1.0)
    mult = jnp.where(v > 0.0, v * lax.rsqrt(v), 0.0)
    return a, mult * (gi * xc_n)


def _lru_out(h, y, x, wo_ref, bo_ref, g_ref, b_ref):
    out = _dot((h * y).astype(BF16), wo_ref[...]) + bo_ref[...]
    return _layer_norm(ALPHA * x + out, g_ref[...], b_ref[...])


def _shift_down_one(z, carry_row):
    rolled = pltpu.roll(z, 1, axis=0)
    row = lax.broadcasted_iota(jnp.int32, (SUBLANES, z.shape[1]), 0)
    first = jnp.where(row == 0, carry_row, rolled[0:SUBLANES, :])
    return jnp.concatenate([first, rolled[SUBLANES:, :]], axis=0)


def _layer0_prompt_kernel(x_ref, wxy_ref, bxy_ref, cw_ref, cb_ref, wg_ref, bg_ref, lam_ref,
                          wo_ref, bo_ref, g1_ref, b1_ref, wup_ref, wdn_ref, g2_ref, b2_ref,
                          y_ref, hout_ref, cout_ref,
                          a_scr, b_scr, h_scr, y_scr, hc_scr, cc_scr, ul_scr, x1_scr, *, tiles_per_seq):
    tt = LRU_TT
    s = pl.program_id(0)
    n_tiles = pl.num_programs(0) - 1
    t = s % tiles_per_seq
    slot = s % 2

    @pl.when(s == 0)
    def _():
        x1_scr[1] = jnp.zeros((tt, D_MODEL), F32)

    @pl.when(t == 0)
    def _():
        hc_scr[...] = jnp.zeros(hc_scr.shape, F32)
        cc_scr[...] = jnp.zeros(cc_scr.shape, F32)

    x1 = x1_scr[1 - slot]
    x1b = x1.astype(BF16)
    acc = jnp.zeros((tt, D_MODEL), F32)
    x = x_ref[...]
    xb = x.astype(BF16)
    nla = _neg_c_softplus_neg(lam_ref[...])
    fc = D_FF // LRU_BLOCKS
    cw2 = 2 * LRU_BLOCK
    def mlp_up(n):
        h = _dot(x1b, wup_ref[:, n * fc:(n + 1) * fc])
        return jnp.square(jnp.maximum(h, 0.0)).astype(BF16)

    def mlp_down(acc, h, n):
        return acc + _dot(h, wdn_ref[n * fc:(n + 1) * fc, :])

    def in_proj(j):
        c0 = j * cw2
        u = _dot(xb, wxy_ref[:, c0:c0 + cw2]) + bxy_ref[:, c0:c0 + cw2]
        y_pre = _dot(xb, wxy_ref[:, D_RNN + c0:D_RNN + c0 + cw2]) + bxy_ref[:, D_RNN + c0:D_RNN + c0 + cw2]
        return u, y_pre

    n_col = D_RNN // cw2
    u, y_pre = in_proj(0)
    h_even = mlp_up(0)
    for j in range(n_col):
        c0 = j * cw2
        h_odd = mlp_up(2 * j + 1)
        acc = mlp_down(acc, h_even, 2 * j)
        y_scr[:, c0:c0 + cw2] = _gelu_tanh(y_pre)
        z = cw_ref[0:1, c0:c0 + cw2] * u
        for k in range(1, CONV_W):
            carry = cc_scr[k - 1:k, c0:c0 + cw2]
            cc_scr[k - 1:k, c0:c0 + cw2] = z[tt - 1:tt, :]
            z = cw_ref[k:k + 1, c0:c0 + cw2] * u + _shift_down_one(z, carry)
        xc = z + cb_ref[:, c0:c0 + cw2]
        ul_scr[:, c0:c0 + cw2] = u[tt - SUBLANES:, :]
        xcs = [xc[:, q * LRU_BLOCK:(q + 1) * LRU_BLOCK] for q in range(2)]
        gts = [_dot(xcs[q].astype(BF16), wg_ref[2 * j + q]) for q in range(2)]
        if j + 1 < n_col:
            u, y_pre = in_proj(j + 1)
            h_even = mlp_up(2 * j + 2)
        acc = mlp_down(acc, h_odd, 2 * j + 1)
        for q in range(2):
            n = 2 * j + q
            lo = n * LRU_BLOCK
            a, b = _lru_gate_math(gts[q], xcs[q], n, bg_ref, nla)
            a_scr[:, lo:lo + LRU_BLOCK] = a
            b_scr[:, lo:lo + LRU_BLOCK] = b

    def step(i, h):
        h = a_scr[pl.ds(i, 1), :] * h + b_scr[pl.ds(i, 1), :]
        h_scr[pl.ds(i, 1), :] = h
        return h

    h_last = lax.fori_loop(0, tt, step, hc_scr[0:1, :], unroll=8)
    hc_scr[0:1, :] = h_last

    y_ref[...] = _layer_norm(ALPHA * x1 + acc, g2_ref[...], b2_ref[...])
    x1_scr[slot] = _lru_out(h_scr[...], y_scr[...], x, wo_ref, bo_ref, g1_ref, b1_ref)

    @pl.when(jnp.logical_and(t == tiles_per_seq - 1, s < n_tiles))
    def _():
        hout_ref[...] = h_last
        cout_ref[...] = ul_scr[SUBLANES - (CONV_W - 1):, :]


def _lru_sample_kernel(x_ref, h0_ref, cs_ref, wxy_ref, bxy_ref, cw_ref, cb_ref, wg_ref, bg_ref,
                       lam_ref, wo_ref, bo_ref, g_ref, b_ref,
                       y_ref, hout_ref, cout_ref, *, nb, nt):
    x = x_ref[...]
    u, y = _lru_in_proj(x, wxy_ref, bxy_ref)
    upad = jnp.concatenate([cs_ref[...], u], axis=0)
    xc = cb_ref[...]
    for k in range(CONV_W):
        xc = xc + cw_ref[k:k + 1, :] * upad[k * nb:(k + nt) * nb, :]
    nla = _neg_c_softplus_neg(lam_ref[...])
    a_parts, b_parts = [], []
    for n in range(LRU_BLOCKS):
        lo = n * LRU_BLOCK
        a, b = _lru_gate_chunk(xc[:, lo:lo + LRU_BLOCK], n, wg_ref, bg_ref, nla)
        a_parts.append(a)
        b_parts.append(b)
    a = jnp.concatenate(a_parts, axis=1)
    b = jnp.concatenate(b_parts, axis=1)
    h = h0_ref[...]
    hs = []
    for t in range(nt):
        h = a[t * nb:(t + 1) * nb, :] * h + b[t * nb:(t + 1) * nb, :]
        hs.append(h)
    y_ref[...] = _lru_out(jnp.concatenate(hs, axis=0), y, x, wo_ref, bo_ref, g_ref, b_ref)
    hout_ref[...] = h
    cout_ref[...] = upad[nt * nb:, :]


def _lru_weight_specs():
    return [_const_spec((D_MODEL, 2 * D_RNN)), _const_spec((1, 2 * D_RNN)),
            _const_spec((CONV_W, D_RNN)), _const_spec((1, D_RNN)),
            _const_spec((LRU_BLOCKS, LRU_BLOCK, 2 * LRU_BLOCK)), _const_spec((LRU_BLOCKS, 1, 2 * LRU_BLOCK)),
            _const_spec((1, D_RNN)), _const_spec((D_RNN, D_MODEL)), _const_spec((1, D_MODEL)),
            _const_spec((1, D_MODEL)), _const_spec((1, D_MODEL))]


def _layer0_prompt(x, lru_weights, mlp_weights):
    bsz, seq, _ = x.shape
    tt = LRU_TT
    assert seq % tt == 0
    tps = seq // tt
    n_tiles = bsz * tps

    def lru_tile(s):
        s = jnp.minimum(s, n_tiles - 1)
        return (s // tps, s % tps, 0)

    def mlp_tile(s):
        s = jnp.maximum(s - 1, 0)
        return (s // tps, s % tps, 0)

    def state_block(s):
        return (jnp.minimum(s, n_tiles - 1) // tps, 0, 0)

    return pl.pallas_call(
        functools.partial(_layer0_prompt_kernel, tiles_per_seq=tps),
        grid=(n_tiles + 1,),
        in_specs=[pl.BlockSpec((None, tt, D_MODEL), lru_tile)] + _lru_weight_specs() + _mlp_weight_specs(),
        out_specs=[pl.BlockSpec((None, tt, D_MODEL), mlp_tile),
                   pl.BlockSpec((None, 1, D_RNN), state_block),
                   pl.BlockSpec((None, CONV_W - 1, D_RNN), state_block)],
        out_shape=[jax.ShapeDtypeStruct((bsz, seq, D_MODEL), F32),
                   jax.ShapeDtypeStruct((bsz, 1, D_RNN), F32),
                   jax.ShapeDtypeStruct((bsz, CONV_W - 1, D_RNN), F32)],
        scratch_shapes=[pltpu.VMEM((tt, D_RNN), F32),
                        pltpu.VMEM((tt, D_RNN), F32),
                        pltpu.VMEM((tt, D_RNN), F32),
                        pltpu.VMEM((tt, D_RNN), F32),
                        pltpu.VMEM((SUBLANES, D_RNN), F32),
                        pltpu.VMEM((SUBLANES, D_RNN), F32),
                        pltpu.VMEM((SUBLANES, D_RNN), F32),
                        pltpu.VMEM((2, tt, D_MODEL), F32)],
        compiler_params=pltpu.CompilerParams(
            dimension_semantics=("arbitrary",), vmem_limit_bytes=VMEM_LIMIT),
        name="layer0_prompt",
    )(x, *lru_weights, *mlp_weights)


def _lru_sample(x_tm, h0, cs_tm, weights, nb, nt):
    rows = nb * nt
    return pl.pallas_call(
        functools.partial(_lru_sample_kernel, nb=nb, nt=nt),
        grid=(1,),
        in_specs=[_const_spec((rows, D_MODEL)), _const_spec((nb, D_RNN)),
                  _const_spec(((CONV_W - 1) * nb, D_RNN))] + _lru_weight_specs(),
        out_specs=[_const_spec((rows, D_MODEL)), _const_spec((nb, D_RNN)),
                   _const_spec(((CONV_W - 1) * nb, D_RNN))],
        out_shape=[jax.ShapeDtypeStruct((rows, D_MODEL), F32),
                   jax.ShapeDtypeStruct((nb, D_RNN), F32),
                   jax.ShapeDtypeStruct(((CONV_W - 1) * nb, D_RNN), F32)],
        compiler_params=pltpu.CompilerParams(
            dimension_semantics=("arbitrary",), vmem_limit_bytes=VMEM_LIMIT),
        name="lru_sample",
    )(x_tm, h0, cs_tm, *weights)


def _lane_is_kv0(shape):
    return lax.broadcasted_iota(jnp.int32, shape, len(shape) - 1) < HEAD_DIM


def _sink_softmax_pv(s, mask, sink, vb):
    s = jnp.where(mask, s, -jnp.inf)
    m = jnp.maximum(jnp.max(s, axis=-1, keepdims=True), sink)
    p = jnp.exp(s - m)
    denom = jnp.sum(p, axis=-1, keepdims=True) + jnp.exp(sink - m)
    return _dot(p.astype(BF16), vb) / denom


def _swa_prompt_kernel(sink_ref, x_ref, wqkv_ref, bqkv_ref, wo_ref, bo_ref, g_ref, b_ref,
                       y_ref, kout_ref, vout_ref,
                       q_scr, k0_scr, k1_scr, v_scr, o_scr):
    tt = SWA_TT
    blk = WINDOW
    t = pl.program_id(1)

    @pl.when(t == 0)
    def _():
        zeros = jnp.zeros((blk, KV_DIM), BF16)
        k0_scr[0:blk, :] = zeros
        k1_scr[0:blk, :] = zeros
        v_scr[0:blk, :] = zeros

    x = x_ref[...]
    qkv = _dot(x.astype(BF16), wqkv_ref[...]) + bqkv_ref[...]
    k = qkv[:, Q_DIM:Q_DIM + KV_DIM]
    v = qkv[:, Q_DIM + KV_DIM:]
    q_scr[...] = (qkv[:, :Q_DIM] * (HEAD_DIM ** -0.5)).astype(BF16)
    kv0 = _lane_is_kv0((tt, KV_DIM))
    k0_scr[blk:, :] = jnp.where(kv0, k, 0.0).astype(BF16)
    k1_scr[blk:, :] = jnp.where(kv0, 0.0, k).astype(BF16)
    v_scr[blk:, :] = v.astype(BF16)

    qi = lax.broadcasted_iota(jnp.int32, (blk, 2 * blk), 0)
    kj = lax.broadcasted_iota(jnp.int32, (blk, 2 * blk), 1)
    band = (kj > qi) & (kj <= qi + blk)
    kv0_o = _lane_is_kv0((blk, KV_DIM))

    for i in range(tt // blk):
        first = jnp.logical_and(t == 0, i == 0)
        mask = band & (kj >= jnp.where(first, blk, 0))
        r0 = i * blk
        qs = jnp.concatenate(
            [q_scr[r0:r0 + blk, g * KV_DIM:(g + 1) * KV_DIM] for g in range(GROUP)], axis=0)
        vb = v_scr[r0:r0 + 2 * blk, :]
        outs = []
        for kh, k_scr in enumerate((k0_scr, k1_scr)):
            s = _dot_nt(qs, k_scr[r0:r0 + 2 * blk, :])
            outs.append([
                _sink_softmax_pv(s[g * blk:(g + 1) * blk, :], mask, sink_ref[kh * GROUP + g], vb)
                for g in range(GROUP)])
        for g in range(GROUP):
            o_scr[r0:r0 + blk, g * KV_DIM:(g + 1) * KV_DIM] = jnp.where(kv0_o, outs[0][g], outs[1][g])

    out = _dot(o_scr[...].astype(BF16), wo_ref[...]) + bo_ref[...]
    y_ref[...] = _layer_norm(ALPHA * x + out, g_ref[...], b_ref[...])

    k0_scr[0:blk, :] = k0_scr[tt:tt + blk, :]
    k1_scr[0:blk, :] = k1_scr[tt:tt + blk, :]
    v_scr[0:blk, :] = v_scr[tt:tt + blk, :]

    @pl.when(t == pl.num_programs(1) - 1)
    def _():
        kout_ref[...] = k[tt - blk:, :]
        vout_ref[...] = v[tt - blk:, :]


def _swa_weight_specs():
    return [_const_spec((D_MODEL, QKV_DIM)), _const_spec((1, QKV_DIM)),
            _const_spec((Q_DIM, D_MODEL)), _const_spec((1, D_MODEL)),
            _const_spec((1, D_MODEL)), _const_spec((1, D_MODEL))]


def _swa_prompt(x, sinks, weights):
    bsz, seq, _ = x.shape
    tt = SWA_TT
    assert seq % tt == 0 and seq >= WINDOW
    tile = pl.BlockSpec((None, tt, D_MODEL), lambda b, t: (b, t, 0))
    kv_spec = pl.BlockSpec((None, WINDOW, KV_DIM), lambda b, t: (b, 0, 0))
    return pl.pallas_call(
        _swa_prompt_kernel,
        grid=(bsz, seq // tt),
        in_specs=[pl.BlockSpec(memory_space=pltpu.SMEM), tile] + _swa_weight_specs(),
        out_specs=[tile, kv_spec, kv_spec],
        out_shape=[jax.ShapeDtypeStruct((bsz, seq, D_MODEL), F32),
                   jax.ShapeDtypeStruct((bsz, WINDOW, KV_DIM), F32),
                   jax.ShapeDtypeStruct((bsz, WINDOW, KV_DIM), F32)],
        scratch_shapes=[pltpu.VMEM((tt, Q_DIM), BF16),
                        pltpu.VMEM((tt + WINDOW, KV_DIM), BF16),
                        pltpu.VMEM((tt + WINDOW, KV_DIM), BF16),
                        pltpu.VMEM((tt + WINDOW, KV_DIM), BF16),
                        pltpu.VMEM((tt, Q_DIM), F32)],
        compiler_params=pltpu.CompilerParams(
            dimension_semantics=("arbitrary", "arbitrary"), vmem_limit_bytes=VMEM_LIMIT),
        name="swa_prompt",
    )(sinks, x, *weights)


def _swa_sample_kernel(sink_ref, x_ref, ck_ref, cv_ref, wqkv_ref, bqkv_ref, wo_ref, bo_ref, g_ref, b_ref,
                       y_ref, knew_ref, vnew_ref,
                       q_scr, kb_scr, vb_scr, o_scr, *, nt):
    pt = SMP_PAD_T
    wbuf = ck_ref.shape[1]
    nkeys = kb_scr.shape[0]
    step = pl.program_id(0)

    @pl.when(step == 0)
    def _():
        qkv = _dot(x_ref[...].astype(BF16), wqkv_ref[...]) + bqkv_ref[...]
        q_scr[...] = qkv[:, :Q_DIM] * (HEAD_DIM ** -0.5)
        knew_ref[...] = qkv[:, Q_DIM:Q_DIM + KV_DIM]
        vnew_ref[...] = qkv[:, Q_DIM + KV_DIM:]
        kb_scr[...] = jnp.zeros(kb_scr.shape, F32)
        vb_scr[...] = jnp.zeros(vb_scr.shape, F32)

    rows = GROUP * pt
    ri = lax.broadcasted_iota(jnp.int32, (rows, nkeys), 0)
    kj = lax.broadcasted_iota(jnp.int32, (rows, nkeys), 1)
    tq = ri % pt
    in_buf = kj < wbuf
    mask = (in_buf & (kj - wbuf + WINDOW > tq)) | (
        jnp.logical_not(in_buf) & (kj - wbuf <= tq) & (kj - wbuf < nt))
    gi = lax.broadcasted_iota(jnp.int32, (rows, 1), 0) // pt
    sink_cols = []
    for kh in range(N_KV):
        col = jnp.zeros((rows, 1), F32)
        for g in range(GROUP):
            col = jnp.where(gi == g, sink_ref[kh * GROUP + g], col)
        sink_cols.append(col)
    kv0_k = _lane_is_kv0((nkeys, KV_DIM))
    kv0_o = _lane_is_kv0((rows, KV_DIM))

    def per_batch(b, carry):
        r0 = pl.multiple_of((step * SMP_BB + b) * pt, pt)
        kb_scr[0:wbuf, :] = ck_ref[b]
        vb_scr[0:wbuf, :] = cv_ref[b]
        kb_scr[wbuf:wbuf + pt, :] = knew_ref[pl.ds(r0, pt), :]
        vb_scr[wbuf:wbuf + pt, :] = vnew_ref[pl.ds(r0, pt), :]
        qs = jnp.concatenate(
            [q_scr[pl.ds(r0, pt), g * KV_DIM:(g + 1) * KV_DIM] for g in range(GROUP)], axis=0)
        qs = qs.astype(BF16)
        kb = kb_scr[...]
        vb = vb_scr[...].astype(BF16)
        o0 = _sink_softmax_pv(_dot_nt(qs, jnp.where(kv0_k, kb, 0.0).astype(BF16)), mask, sink_cols[0], vb)
        o1 = _sink_softmax_pv(_dot_nt(qs, jnp.where(kv0_k, 0.0, kb).astype(BF16)), mask, sink_cols[1], vb)
        o = jnp.where(kv0_o, o0, o1)
        for g in range(GROUP):
            o_scr[pl.ds(r0, pt), g * KV_DIM:(g + 1) * KV_DIM] = o[g * pt:(g + 1) * pt, :]
        return carry

    lax.fori_loop(0, SMP_BB, per_batch, 0)

    @pl.when(step == pl.num_programs(0) - 1)
    def _():
        out = _dot(o_scr[...].astype(BF16), wo_ref[...]) + bo_ref[...]
        y_ref[...] = _layer_norm(ALPHA * x_ref[...] + out, g_ref[...], b_ref[...])


def _swa_sample(x_pad, cache_k, cache_v, sinks, weights, nt):
    rows = x_pad.shape[0]
    nb, wbuf, _ = cache_k.shape
    assert nb % SMP_BB == 0 and rows == nb * SMP_PAD_T and nt <= SMP_PAD_T
    nkeys = 2 * WINDOW
    assert wbuf + SMP_PAD_T <= nkeys
    cache_spec = pl.BlockSpec((SMP_BB, wbuf, KV_DIM), lambda i: (i, 0, 0))
    return pl.pallas_call(
        functools.partial(_swa_sample_kernel, nt=nt),
        grid=(nb // SMP_BB,),
        in_specs=[pl.BlockSpec(memory_space=pltpu.SMEM), _const_spec((rows, D_MODEL)),
                  cache_spec, cache_spec] + _swa_weight_specs(),
        out_specs=[_const_spec((rows, D_MODEL)), _const_spec((rows, KV_DIM)), _const_spec((rows, KV_DIM))],
        out_shape=[jax.ShapeDtypeStruct((rows, D_MODEL), F32),
                   jax.ShapeDtypeStruct((rows, KV_DIM), F32),
                   jax.ShapeDtypeStruct((rows, KV_DIM), F32)],
        scratch_shapes=[pltpu.VMEM((rows, Q_DIM), F32),
                        pltpu.VMEM((nkeys, KV_DIM), F32),
                        pltpu.VMEM((nkeys, KV_DIM), F32),
                        pltpu.VMEM((rows, Q_DIM), F32)],
        compiler_params=pltpu.CompilerParams(
            dimension_semantics=("arbitrary",), vmem_limit_bytes=VMEM_LIMIT),
        name="swa_sample",
    )(sinks, x_pad, cache_k, cache_v, *weights)


def _row(v):
    return v.reshape(1, -1).astype(F32)


def _pack_lru_weights(j, lru_w_x, lru_b_x, lru_w_y, lru_b_y, lru_conv_w, lru_conv_b,
                      lru_w_ga, lru_b_ga, lru_w_gi, lru_b_gi, lru_lam, lru_w_out, lru_b_out, g, b):
    wxy = jnp.concatenate([lru_w_x[j], lru_w_y[j]], axis=1).astype(BF16)
    bxy = jnp.concatenate([lru_b_x[j], lru_b_y[j]]).reshape(1, -1)
    wg = jnp.concatenate([lru_w_ga[j], lru_w_gi[j]], axis=2).astype(BF16)
    bg = jnp.concatenate([lru_b_ga[j], lru_b_gi[j]], axis=1)[:, None, :]
    return (wxy, bxy, lru_conv_w[j], _row(lru_conv_b[j]), wg, bg, _row(lru_lam[j]),
            lru_w_out[j].astype(BF16), _row(lru_b_out[j]), _row(g), _row(b))


def _pack_swa_weights(j, swa_w_qkv, swa_b_qkv, swa_w_o, swa_b_o, g, b):
    w, bias = swa_w_qkv[j], swa_b_qkv[j]
    wq = w[:, :Q_DIM].reshape(D_MODEL, N_KV, GROUP, HEAD_DIM).transpose(0, 2, 1, 3).reshape(D_MODEL, Q_DIM)
    bq = bias[:Q_DIM].reshape(N_KV, GROUP, HEAD_DIM).transpose(1, 0, 2).reshape(Q_DIM)
    wqkv = jnp.concatenate([wq, w[:, Q_DIM:]], axis=1).astype(BF16)
    bqkv = jnp.concatenate([bq, bias[Q_DIM:]]).reshape(1, -1)
    wo = swa_w_o[j].reshape(N_KV, GROUP, HEAD_DIM, D_MODEL).transpose(1, 0, 2, 3).reshape(Q_DIM, D_MODEL)
    return (wqkv, bqkv, wo.astype(BF16), _row(swa_b_o[j]), _row(g), _row(b))


def kernel(x_prompt, x_sample, state_lru_h, state_lru_conv, cache_swa_k, cache_swa_v,
           lru_w_x, lru_b_x, lru_w_y, lru_b_y, lru_conv_w, lru_conv_b,
           lru_w_ga, lru_b_ga, lru_w_gi, lru_b_gi, lru_lam, lru_w_out, lru_b_out,
           swa_w_qkv, swa_b_qkv, swa_sinks, swa_w_o, swa_b_o,
           mlp_w_up, mlp_w_down, ln1_g, ln1_b, ln2_g, ln2_b):
    bsz, seq, _ = x_prompt.shape
    nb, nt, _ = x_sample.shape
    wbuf = cache_swa_k.shape[2]

    mlp_w = [(mlp_w_up[i].astype(BF16), mlp_w_down[i].astype(BF16), _row(ln2_g[i]), _row(ln2_b[i]))
             for i in range(DEPTH)]

    def mlp(x2d, i):
        return _mlp_block(x2d, *mlp_w[i])

    lw = _pack_lru_weights(0, lru_w_x, lru_b_x, lru_w_y, lru_b_y, lru_conv_w, lru_conv_b,
                           lru_w_ga, lru_b_ga, lru_w_gi, lru_b_gi, lru_lam, lru_w_out, lru_b_out,
                           ln1_g[0], ln1_b[0])
    xp, h_p, conv_p = _layer0_prompt(x_prompt, lw, mlp_w[0])

    xs_tm = x_sample.transpose(1, 0, 2).reshape(nt * nb, D_MODEL)
    cs_tm = state_lru_conv[0].transpose(1, 0, 2).reshape((CONV_W - 1) * nb, D_RNN)
    xs_tm, h_s, conv_s_tm = _lru_sample(xs_tm, state_lru_h[0], cs_tm, lw, nb, nt)
    xs_tm = mlp(xs_tm, 0)
    conv_s = conv_s_tm.reshape(CONV_W - 1, nb, D_RNN).transpose(1, 0, 2)

    sw = _pack_swa_weights(0, swa_w_qkv, swa_b_qkv, swa_w_o, swa_b_o, ln1_g[1], ln1_b[1])
    sinks = swa_sinks[0].astype(F32)
    xp, k_p, v_p = _swa_prompt(xp.reshape(bsz, seq, D_MODEL), sinks, sw)
    xp = mlp(xp.reshape(bsz * seq, D_MODEL), 1).reshape(bsz, seq, D_MODEL)

    xs_bm = xs_tm.reshape(nt, nb, D_MODEL).transpose(1, 0, 2)
    xs_pad = jnp.pad(xs_bm, ((0, 0), (0, SMP_PAD_T - nt), (0, 0))).reshape(nb * SMP_PAD_T, D_MODEL)
    ck = cache_swa_k[0].reshape(nb, wbuf, KV_DIM)
    cv = cache_swa_v[0].reshape(nb, wbuf, KV_DIM)
    ys_pad, k_new, v_new = _swa_sample(xs_pad, ck, cv, sinks, sw, nt)
    xs = ys_pad.reshape(nb, SMP_PAD_T, D_MODEL)[:, :nt].reshape(nb * nt, D_MODEL)
    xs = mlp(xs, 1).reshape(nb, nt, D_MODEL)
    k_new = k_new.reshape(nb, SMP_PAD_T, KV_DIM)[:, :nt]
    v_new = v_new.reshape(nb, SMP_PAD_T, KV_DIM)[:, :nt]
    k_s = jnp.concatenate([ck, k_new], axis=1)[:, nt:]
    v_s = jnp.concatenate([cv, v_new], axis=1)[:, nt:]

    kv_shape = (1, -1, min(WINDOW, seq), N_KV, HEAD_DIM)
    return (xp, xs,
            h_p.reshape(1, bsz, D_RNN), conv_p[None],
            k_p.reshape(kv_shape), v_p.reshape(kv_shape),
            h_s[None], conv_s[None],
            k_s.reshape(1, nb, wbuf, N_KV, HEAD_DIM), v_s.reshape(1, nb, wbuf, N_KV, HEAD_DIM))
```

```python
import functools

import jax
import jax.numpy as jnp
from jax import lax
from jax.experimental import pallas as pl
from jax.experimental.pallas import tpu as pltpu

F32 = jnp.float32
BF16 = jnp.bfloat16

D_MODEL = 1024
D_RNN = 1024
D_FF = 4096
LRU_BLOCKS = 8
LRU_BLOCK = 128
CONV_W = 4
LRU_C = 8.0
N_HEADS = 16
HEAD_DIM = 64
N_KV = 2
GROUP = 8
WINDOW = 128
KV_DIM = N_KV * HEAD_DIM
Q_DIM = N_HEADS * HEAD_DIM
QKV_DIM = Q_DIM + 2 * KV_DIM
DEPTH = 2
ALPHA = (2.0 * DEPTH) ** 0.25
LN_EPS = 1e-5

SUBLANES = 8
VMEM_LIMIT = 48 * 1024 * 1024

MLP_TM = 512
MLP_FC = 1024
LRU_TT = 256
SWA_TT = 512
SMP_PAD_T = 8
SMP_BB = 16


def _const_spec(shape):
    nd = len(shape)
    return pl.BlockSpec(shape, lambda *_: (0,) * nd, pipeline_mode=pl.Buffered(1))


def _dot(a, b):
    return jnp.dot(a, b, preferred_element_type=F32)


def _dot_nt(a, b):
    return lax.dot_general(a, b, (((1,), (1,)), ((), ())), preferred_element_type=F32)


def _layer_norm(z, g, b):
    mu = jnp.mean(z, axis=-1, keepdims=True)
    d = z - mu
    var = jnp.mean(d * d, axis=-1, keepdims=True)
    return d * lax.rsqrt(var + LN_EPS) * g + b


def _sigmoid(x):
    return 0.5 * jnp.tanh(0.5 * x) + 0.5


def _mlp_ln(x, wup_ref, wdn_ref, g_ref, b_ref):
    xb = x.astype(BF16)
    acc = jnp.zeros(x.shape, F32)
    for j in range(D_FF // MLP_FC):
        h = _dot(xb, wup_ref[:, j * MLP_FC:(j + 1) * MLP_FC])
        h = jnp.square(jnp.maximum(h, 0.0)).astype(BF16)
        acc = acc + _dot(h, wdn_ref[j * MLP_FC:(j + 1) * MLP_FC, :])
    return _layer_norm(ALPHA * x + acc, g_ref[...], b_ref[...])


def _mlp_kernel(x_ref, wup_ref, wdn_ref, g_ref, b_ref, o_ref):
    o_ref[...] = _mlp_ln(x_ref[...], wup_ref, wdn_ref, g_ref, b_ref)


def _mlp_weight_specs():
    return [_const_spec((D_MODEL, D_FF)), _const_spec((D_FF, D_MODEL)),
            _const_spec((1, D_MODEL)), _const_spec((1, D_MODEL))]


def _mlp_block(x, w_up, w_dn, g, b):
    n = x.shape[0]
    tm = min(MLP_TM, n)
    assert n % tm == 0
    row_spec = pl.BlockSpec((tm, D_MODEL), lambda i: (i, 0))
    return pl.pallas_call(
        _mlp_kernel,
        grid=(n // tm,),
        in_specs=[row_spec] + _mlp_weight_specs(),
        out_specs=row_spec,
        out_shape=jax.ShapeDtypeStruct((n, D_MODEL), F32),
        compiler_params=pltpu.CompilerParams(
            dimension_semantics=("arbitrary",), vmem_limit_bytes=VMEM_LIMIT),
        name="mlp_ln",
    )(x, w_up, w_dn, g, b)


_GELU_C0 = (2.0 / jnp.pi) ** 0.5
_GELU_C1 = _GELU_C0 * 0.044715


def _gelu_tanh(x):
    t = jnp.tanh(x * (_GELU_C0 + _GELU_C1 * (x * x)))
    hx = 0.5 * x
    return hx + hx * t


def _lru_in_proj(x, wxy_ref, bxy_ref):
    uy = _dot(x.astype(BF16), wxy_ref[...]) + bxy_ref[...]
    u = uy[:, :D_RNN]
    y = _gelu_tanh(uy[:, D_RNN:])
    return u, y


def _neg_c_softplus_neg(lam):
    z = -lam
    sp = jnp.maximum(z, 0.0) + jnp.log1p(jnp.exp(-jnp.abs(z)))
    return -LRU_C * sp


def _lru_gate_chunk(xc_n, n, wg_ref, bg_ref, nla):
    return _lru_gate_math(_dot(xc_n.astype(BF16), wg_ref[n]), xc_n, n, bg_ref, nla)


def _lru_gate_math(gt, xc_n, n, bg_ref, nla):
    lo = n * LRU_BLOCK
    gt = gt + bg_ref[n]
    r = _sigmoid(gt[:, :LRU_BLOCK])
    gi = _sigmoid(gt[:, LRU_BLOCK:])
    log_a = r * nla[:, lo:lo + LRU_BLOCK]
    a = jnp.exp(log_a)
    v = -jnp.tanh(log_a) * (a * a + 1.0)
    mult = jnp.where(v > 0.0, v * lax.rsqrt(v), 0.0)
    return a, mult * (gi * xc_n)


def _lru_out(h, y, x, wo_ref, bo_ref, g_ref, b_ref):
    out = _dot((h * y).astype(BF16), wo_ref[...]) + bo_ref[...]
    return _layer_norm(ALPHA * x + out, g_ref[...], b_ref[...])


def _interleave_matrix(tt):
    r = jnp.arange(tt)
    src_time = (r % SUBLANES) * (tt // SUBLANES) + r // SUBLANES
    return (src_time[:, None] == jnp.arange(tt)[None, :]).astype(BF16)


def _shift_one_step(z, carry_row):
    tt = z.shape[0]
    last = z[tt - SUBLANES:, :]
    row = lax.broadcasted_iota(jnp.int32, last.shape, 0)
    first = jnp.where(row == 0, carry_row, pltpu.roll(last, 1, axis=0))
    return jnp.concatenate([first, z[:tt - SUBLANES, :]], axis=0)


def _scan_interleaved(a, b, h_in):
    n_steps = a.shape[0] // SUBLANES
    h = b[0:SUBLANES, :]
    p = a[0:SUBLANES, :]
    hs, ps = [h], [p]
    for i in range(1, n_steps):
        ai = a[i * SUBLANES:(i + 1) * SUBLANES, :]
        h = ai * h + b[i * SUBLANES:(i + 1) * SUBLANES, :]
        p = ai * p
        hs.append(h)
        ps.append(p)
    row = lax.broadcasted_iota(jnp.int32, h.shape, 0)
    cin = h_in
    cvec = jnp.zeros(h.shape, F32)
    for c in range(SUBLANES):
        cvec = jnp.where(row == c, cin, cvec)
        cin = h[c:c + 1, :] + p[c:c + 1, :] * cin
    full = jnp.concatenate([hs[i] + ps[i] * cvec for i in range(n_steps)], axis=0)
    return full, cin


def _layer0_prompt_kernel(x_ref, pm_ref, pmt_ref, wxy_ref, bxy_ref, cw_ref, cb_ref, wg_ref, bg_ref, lam_ref,
                          wo_ref, bo_ref, g1_ref, b1_ref, wup_ref, wdn_ref, g2_ref, b2_ref,
                          y_ref, hout_ref, cout_ref,
                          hy_scr, hc_scr, cc_scr, ul_scr, x1_scr, *, tiles_per_seq):
    tt = LRU_TT
    s = pl.program_id(0)
    n_tiles = pl.num_programs(0) - 1
    t = s % tiles_per_seq
    slot = s % 2

    @pl.when(s == 0)
    def _():
        x1_scr[1] = jnp.zeros((tt, D_MODEL), F32)

    @pl.when(t == 0)
    def _():
        hc_scr[...] = jnp.zeros(hc_scr.shape, F32)
        cc_scr[...] = jnp.zeros(cc_scr.shape, F32)

    x1 = x1_scr[1 - slot]
    x1b = x1.astype(BF16)
    acc = jnp.zeros((tt, D_MODEL), F32)
    x = x_ref[...]
    xb = x.astype(BF16)
    nla = _neg_c_softplus_neg(lam_ref[...])
    n_mlp = LRU_BLOCKS
    fc = D_FF // n_mlp
    cw2 = 2 * LRU_BLOCK
    n_col = D_RNN // cw2

    def mlp_up(n):
        h = _dot(x1b, wup_ref[:, n * fc:(n + 1) * fc])
        return jnp.square(jnp.maximum(h, 0.0)).astype(BF16)

    def mlp_down(acc, h, n):
        return acc + _dot(h, wdn_ref[n * fc:(n + 1) * fc, :])

    def in_proj(j):
        return _dot(xpb, wxy_ref[:, 2 * j * cw2:2 * (j + 1) * cw2]) + bxy_ref[:, 2 * j * cw2:2 * (j + 1) * cw2]

    xp = _dot(pm_ref[...], xb)
    hs = {0: mlp_up(0)}
    xpb = xp.astype(BF16)
    uy = in_proj(0)
    for j in range(n_col):
        c0 = j * cw2
        hs[j + 1] = mlp_up(j + 1)
        acc = mlp_down(acc, hs.pop(j), j)
        u = uy[:, :cw2]
        y = _gelu_tanh(uy[:, cw2:])
        z = cw_ref[0:1, c0:c0 + cw2] * u
        for k in range(1, CONV_W):
            carry = cc_scr[k - 1:k, c0:c0 + cw2]
            cc_scr[k - 1:k, c0:c0 + cw2] = z[tt - 1:tt, :]
            z = cw_ref[k:k + 1, c0:c0 + cw2] * u + _shift_one_step(z, carry)
        xc = z + cb_ref[:, c0:c0 + cw2]
        ul_scr[:, c0:c0 + cw2] = u[tt - (CONV_W - 1) * SUBLANES:, :]
        xcs = [xc[:, q * LRU_BLOCK:(q + 1) * LRU_BLOCK] for q in range(2)]
        gts = [_dot(xcs[q].astype(BF16), wg_ref[2 * j + q]) for q in range(2)]
        if j + 1 < n_col:
            uy = in_proj(j + 1)
        for q in range(2):
            n = 2 * j + q
            lo = n * LRU_BLOCK
            a, b = _lru_gate_math(gts[q], xcs[q], n, bg_ref, nla)
            h, h_end = _scan_interleaved(a, b, hc_scr[0:1, lo:lo + LRU_BLOCK])
            hc_scr[0:1, lo:lo + LRU_BLOCK] = h_end
            hy_scr[:, lo:lo + LRU_BLOCK] = (h * y[:, q * LRU_BLOCK:(q + 1) * LRU_BLOCK]).astype(BF16)

    nxt = n_col + 1
    hs[nxt] = mlp_up(nxt)
    acc = mlp_down(acc, hs.pop(n_col), n_col)
    hs[nxt + 1] = mlp_up(nxt + 1)
    hy = _dot(pmt_ref[...], hy_scr[...])
    acc = mlp_down(acc, hs.pop(nxt), nxt)
    hs[nxt + 2] = mlp_up(nxt + 2)
    out = _dot(hy.astype(BF16), wo_ref[...]) + bo_ref[...]
    acc = mlp_down(acc, hs.pop(nxt + 1), nxt + 1)
    acc = mlp_down(acc, hs.pop(nxt + 2), nxt + 2)
    assert not hs and nxt + 2 == n_mlp - 1
    x1_scr[slot] = _layer_norm(ALPHA * x + out, g1_ref[...], b1_ref[...])
    y_ref[...] = _layer_norm(ALPHA * x1 + acc, g2_ref[...], b2_ref[...])

    @pl.when(jnp.logical_and(t == tiles_per_seq - 1, s < n_tiles))
    def _():
        hout_ref[...] = hc_scr[0:1, :]
        for k in range(CONV_W - 1):
            r = (k + 1) * SUBLANES - 1
            cout_ref[k:k + 1, :] = ul_scr[r:r + 1, :]


def _lru_sample_kernel(x_ref, h0_ref, cs_ref, wxy_ref, bxy_ref, cw_ref, cb_ref, wg_ref, bg_ref,
                       lam_ref, wo_ref, bo_ref, g_ref, b_ref,
                       y_ref, hout_ref, cout_ref, *, nb, nt):
    x = x_ref[...]
    u, y = _lru_in_proj(x, wxy_ref, bxy_ref)
    upad = jnp.concatenate([cs_ref[...], u], axis=0)
    xc = cb_ref[...]
    for k in range(CONV_W):
        xc = xc + cw_ref[k:k + 1, :] * upad[k * nb:(k + nt) * nb, :]
    nla = _neg_c_softplus_neg(lam_ref[...])
    a_parts, b_parts = [], []
    for n in range(LRU_BLOCKS):
        lo = n * LRU_BLOCK
        a, b = _lru_gate_chunk(xc[:, lo:lo + LRU_BLOCK], n, wg_ref, bg_ref, nla)
        a_parts.append(a)
        b_parts.append(b)
    a = jnp.concatenate(a_parts, axis=1)
    b = jnp.concatenate(b_parts, axis=1)
    h = h0_ref[...]
    hs = []
    for t in range(nt):
        h = a[t * nb:(t + 1) * nb, :] * h + b[t * nb:(t + 1) * nb, :]
        hs.append(h)
    y_ref[...] = _lru_out(jnp.concatenate(hs, axis=0), y, x, wo_ref, bo_ref, g_ref, b_ref)
    hout_ref[...] = h
    cout_ref[...] = upad[nt * nb:, :]


def _lru_weight_specs():
    return [_const_spec((D_MODEL, 2 * D_RNN)), _const_spec((1, 2 * D_RNN)),
            _const_spec((CONV_W, D_RNN)), _const_spec((1, D_RNN)),
            _const_spec((LRU_BLOCKS, LRU_BLOCK, 2 * LRU_BLOCK)), _const_spec((LRU_BLOCKS, 1, 2 * LRU_BLOCK)),
            _const_spec((1, D_RNN)), _const_spec((D_RNN, D_MODEL)), _const_spec((1, D_MODEL)),
            _const_spec((1, D_MODEL)), _const_spec((1, D_MODEL))]


def _layer0_prompt(x, lru_weights, mlp_weights):
    bsz, seq, _ = x.shape
    tt = LRU_TT
    assert seq % tt == 0
    tps = seq // tt
    n_tiles = bsz * tps

    def lru_tile(s):
        s = jnp.minimum(s, n_tiles - 1)
        return (s // tps, s % tps, 0)

    def mlp_tile(s):
        s = jnp.maximum(s - 1, 0)
        return (s // tps, s % tps, 0)

    def state_block(s):
        return (jnp.minimum(s, n_tiles - 1) // tps, 0, 0)

    pm = _interleave_matrix(tt)
    return pl.pallas_call(
        functools.partial(_layer0_prompt_kernel, tiles_per_seq=tps),
        grid=(n_tiles + 1,),
        in_specs=[pl.BlockSpec((None, tt, D_MODEL), lru_tile), _const_spec((tt, tt)), _const_spec((tt, tt))]
        + _lru_weight_specs() + _mlp_weight_specs(),
        out_specs=[pl.BlockSpec((None, tt, D_MODEL), mlp_tile),
                   pl.BlockSpec((None, 1, D_RNN), state_block),
                   pl.BlockSpec((None, CONV_W - 1, D_RNN), state_block)],
        out_shape=[jax.ShapeDtypeStruct((bsz, seq, D_MODEL), F32),
                   jax.ShapeDtypeStruct((bsz, 1, D_RNN), F32),
                   jax.ShapeDtypeStruct((bsz, CONV_W - 1, D_RNN), F32)],
        scratch_shapes=[pltpu.VMEM((tt, D_RNN), BF16),
                        pltpu.VMEM((SUBLANES, D_RNN), F32),
                        pltpu.VMEM((SUBLANES, D_RNN), F32),
                        pltpu.VMEM(((CONV_W - 1) * SUBLANES, D_RNN), F32),
                        pltpu.VMEM((2, tt, D_MODEL), F32)],
        compiler_params=pltpu.CompilerParams(
            dimension_semantics=("arbitrary",), vmem_limit_bytes=VMEM_LIMIT),
        name="layer0_prompt",
    )(x, pm, pm.T, *lru_weights, *mlp_weights)


def _lru_sample(x_tm, h0, cs_tm, weights, nb, nt):
    rows = nb * nt
    return pl.pallas_call(
        functools.partial(_lru_sample_kernel, nb=nb, nt=nt),
        grid=(1,),
        in_specs=[_const_spec((rows, D_MODEL)), _const_spec((nb, D_RNN)),
                  _const_spec(((CONV_W - 1) * nb, D_RNN))] + _lru_weight_specs(),
        out_specs=[_const_spec((rows, D_MODEL)), _const_spec((nb, D_RNN)),
                   _const_spec(((CONV_W - 1) * nb, D_RNN))],
        out_shape=[jax.ShapeDtypeStruct((rows, D_MODEL), F32),
                   jax.ShapeDtypeStruct((nb, D_RNN), F32),
                   jax.ShapeDtypeStruct(((CONV_W - 1) * nb, D_RNN), F32)],
        compiler_params=pltpu.CompilerParams(
            dimension_semantics=("arbitrary",), vmem_limit_bytes=VMEM_LIMIT),
        name="lru_sample",
    )(x_tm, h0, cs_tm, *weights)


def _lane_is_kv0(shape):
    return lax.broadcasted_iota(jnp.int32, shape, len(shape) - 1) < HEAD_DIM


def _sink_softmax_pv(s, mask, sink, vb):
    s = jnp.where(mask, s, -jnp.inf)
    m = jnp.maximum(jnp.max(s, axis=-1, keepdims=True), sink)
    p = jnp.exp(s - m)
    denom = jnp.sum(p, axis=-1, keepdims=True) + jnp.exp(sink - m)
    return _dot(p.astype(BF16), vb) / denom


def _swa_prompt_kernel(sink_ref, x_ref, wqkv_ref, bqkv_ref, wo_ref, bo_ref, g_ref, b_ref,
                       y_ref, kout_ref, vout_ref,
                       q_scr, k0_scr, k1_scr, v_scr, o_scr):
    tt = SWA_TT
    blk = WINDOW
    t = pl.program_id(1)

    @pl.when(t == 0)
    def _():
        zeros = jnp.zeros((blk, KV_DIM), BF16)
        k0_scr[0:blk, :] = zeros
        k1_scr[0:blk, :] = zeros
        v_scr[0:blk, :] = zeros

    x = x_ref[...]
    qkv = _dot(x.astype(BF16), wqkv_ref[...]) + bqkv_ref[...]
    k = qkv[:, Q_DIM:Q_DIM + KV_DIM]
    v = qkv[:, Q_DIM + KV_DIM:]
    q_scr[...] = (qkv[:, :Q_DIM] * (HEAD_DIM ** -0.5)).astype(BF16)
    kv0 = _lane_is_kv0((tt, KV_DIM))
    k0_scr[blk:, :] = jnp.where(kv0, k, 0.0).astype(BF16)
    k1_scr[blk:, :] = jnp.where(kv0, 0.0, k).astype(BF16)
    v_scr[blk:, :] = v.astype(BF16)

    qi = lax.broadcasted_iota(jnp.int32, (blk, 2 * blk), 0)
    kj = lax.broadcasted_iota(jnp.int32, (blk, 2 * blk), 1)
    band = (kj > qi) & (kj <= qi + blk)
    kv0_o = _lane_is_kv0((blk, KV_DIM))

    for i in range(tt // blk):
        first = jnp.logical_and(t == 0, i == 0)
        mask = band & (kj >= jnp.where(first, blk, 0))
        r0 = i * blk
        qs = jnp.concatenate(
            [q_scr[r0:r0 + blk, g * KV_DIM:(g + 1) * KV_DIM] for g in range(GROUP)], axis=0)
        vb = v_scr[r0:r0 + 2 * blk, :]
        outs = []
        for kh, k_scr in enumerate((k0_scr, k1_scr)):
            s = _dot_nt(qs, k_scr[r0:r0 + 2 * blk, :])
            outs.append([
                _sink_softmax_pv(s[g * blk:(g + 1) * blk, :], mask, sink_ref[kh * GROUP + g], vb)
                for g in range(GROUP)])
        for g in range(GROUP):
            o_scr[r0:r0 + blk, g * KV_DIM:(g + 1) * KV_DIM] = jnp.where(kv0_o, outs[0][g], outs[1][g])

    out = _dot(o_scr[...].astype(BF16), wo_ref[...]) + bo_ref[...]
    y_ref[...] = _layer_norm(ALPHA * x + out, g_ref[...], b_ref[...])

    k0_scr[0:blk, :] = k0_scr[tt:tt + blk, :]
    k1_scr[0:blk, :] = k1_scr[tt:tt + blk, :]
    v_scr[0:blk, :] = v_scr[tt:tt + blk, :]

    @pl.when(t == pl.num_programs(1) - 1)
    def _():
        kout_ref[...] = k[tt - blk:, :]
        vout_ref[...] = v[tt - blk:, :]


def _swa_weight_specs():
    return [_const_spec((D_MODEL, QKV_DIM)), _const_spec((1, QKV_DIM)),
            _const_spec((Q_DIM, D_MODEL)), _const_spec((1, D_MODEL)),
            _const_spec((1, D_MODEL)), _const_spec((1, D_MODEL))]


def _swa_prompt(x, sinks, weights):
    bsz, seq, _ = x.shape
    tt = SWA_TT
    assert seq % tt == 0 and seq >= WINDOW
    tile = pl.BlockSpec((None, tt, D_MODEL), lambda b, t: (b, t, 0))
    kv_spec = pl.BlockSpec((None, WINDOW, KV_DIM), lambda b, t: (b, 0, 0))
    return pl.pallas_call(
        _swa_prompt_kernel,
        grid=(bsz, seq // tt),
        in_specs=[pl.BlockSpec(memory_space=pltpu.SMEM), tile] + _swa_weight_specs(),
        out_specs=[tile, kv_spec, kv_spec],
        out_shape=[jax.ShapeDtypeStruct((bsz, seq, D_MODEL), F32),
                   jax.ShapeDtypeStruct((bsz, WINDOW, KV_DIM), F32),
                   jax.ShapeDtypeStruct((bsz, WINDOW, KV_DIM), F32)],
        scratch_shapes=[pltpu.VMEM((tt, Q_DIM), BF16),
                        pltpu.VMEM((tt + WINDOW, KV_DIM), BF16),
                        pltpu.VMEM((tt + WINDOW, KV_DIM), BF16),
                        pltpu.VMEM((tt + WINDOW, KV_DIM), BF16),
                        pltpu.VMEM((tt, Q_DIM), F32)],
        compiler_params=pltpu.CompilerParams(
            dimension_semantics=("arbitrary", "arbitrary"), vmem_limit_bytes=VMEM_LIMIT),
        name="swa_prompt",
    )(sinks, x, *weights)


def _swa_sample_kernel(sink_ref, x_ref, ck_ref, cv_ref, wqkv_ref, bqkv_ref, wo_ref, bo_ref, g_ref, b_ref,
                       y_ref, knew_ref, vnew_ref,
                       q_scr, kb_scr, vb_scr, o_scr, *, nt):
    pt = SMP_PAD_T
    wbuf = ck_ref.shape[1]
    nkeys = kb_scr.shape[0]
    step = pl.program_id(0)

    @pl.when(step == 0)
    def _():
        qkv = _dot(x_ref[...].astype(BF16), wqkv_ref[...]) + bqkv_ref[...]
        q_scr[...] = qkv[:, :Q_DIM] * (HEAD_DIM ** -0.5)
        knew_ref[...] = qkv[:, Q_DIM:Q_DIM + KV_DIM]
        vnew_ref[...] = qkv[:, Q_DIM + KV_DIM:]
        kb_scr[...] = jnp.zeros(kb_scr.shape, F32)
        vb_scr[...] = jnp.zeros(vb_scr.shape, F32)

    rows = GROUP * pt
    ri = lax.broadcasted_iota(jnp.int32, (rows, nkeys), 0)
    kj = lax.broadcasted_iota(jnp.int32, (rows, nkeys), 1)
    tq = ri % pt
    in_buf = kj < wbuf
    mask = (in_buf & (kj - wbuf + WINDOW > tq)) | (
        jnp.logical_not(in_buf) & (kj - wbuf <= tq) & (kj - wbuf < nt))
    gi = lax.broadcasted_iota(jnp.int32, (rows, 1), 0) // pt
    sink_cols = []
    for kh in range(N_KV):
        col = jnp.zeros((rows, 1), F32)
        for g in range(GROUP):
            col = jnp.where(gi == g, sink_ref[kh * GROUP + g], col)
        sink_cols.append(col)
    kv0_k = _lane_is_kv0((nkeys, KV_DIM))
    kv0_o = _lane_is_kv0((rows, KV_DIM))

    def per_batch(b, carry):
        r0 = pl.multiple_of((step * SMP_BB + b) * pt, pt)
        kb_scr[0:wbuf, :] = ck_ref[b]
        vb_scr[0:wbuf, :] = cv_ref[b]
        kb_scr[wbuf:wbuf + pt, :] = knew_ref[pl.ds(r0, pt), :]
        vb_scr[wbuf:wbuf + pt, :] = vnew_ref[pl.ds(r0, pt), :]
        qs = jnp.concatenate(
            [q_scr[pl.ds(r0, pt), g * KV_DIM:(g + 1) * KV_DIM] for g in range(GROUP)], axis=0)
        qs = qs.astype(BF16)
        kb = kb_scr[...]
        vb = vb_scr[...].astype(BF16)
        o0 = _sink_softmax_pv(_dot_nt(qs, jnp.where(kv0_k, kb, 0.0).astype(BF16)), mask, sink_cols[0], vb)
        o1 = _sink_softmax_pv(_dot_nt(qs, jnp.where(kv0_k, 0.0, kb).astype(BF16)), mask, sink_cols[1], vb)
        o = jnp.where(kv0_o, o0, o1)
        for g in range(GROUP):
            o_scr[pl.ds(r0, pt), g * KV_DIM:(g + 1) * KV_DIM] = o[g * pt:(g + 1) * pt, :]
        return carry

    lax.fori_loop(0, SMP_BB, per_batch, 0)

    @pl.when(step == pl.num_programs(0) - 1)
    def _():
        out = _dot(o_scr[...].astype(BF16), wo_ref[...]) + bo_ref[...]
        y_ref[...] = _layer_norm(ALPHA * x_ref[...] + out, g_ref[...], b_ref[...])


def _swa_sample(x_pad, cache_k, cache_v, sinks, weights, nt):
    rows = x_pad.shape[0]
    nb, wbuf, _ = cache_k.shape
    assert nb % SMP_BB == 0 and rows == nb * SMP_PAD_T and nt <= SMP_PAD_T
    nkeys = 2 * WINDOW
    assert wbuf + SMP_PAD_T <= nkeys
    cache_spec = pl.BlockSpec((SMP_BB, wbuf, KV_DIM), lambda i: (i, 0, 0))
    return pl.pallas_call(
        functools.partial(_swa_sample_kernel, nt=nt),
        grid=(nb // SMP_BB,),
        in_specs=[pl.BlockSpec(memory_space=pltpu.SMEM), _const_spec((rows, D_MODEL)),
                  cache_spec, cache_spec] + _swa_weight_specs(),
        out_specs=[_const_spec((rows, D_MODEL)), _const_spec((rows, KV_DIM)), _const_spec((rows, KV_DIM))],
        out_shape=[jax.ShapeDtypeStruct((rows, D_MODEL), F32),
                   jax.ShapeDtypeStruct((rows, KV_DIM), F32),
                   jax.ShapeDtypeStruct((rows, KV_DIM), F32)],
        scratch_shapes=[pltpu.VMEM((rows, Q_DIM), F32),
                        pltpu.VMEM((nkeys, KV_DIM), F32),
                        pltpu.VMEM((nkeys, KV_DIM), F32),
                        pltpu.VMEM((rows, Q_DIM), F32)],
        compiler_params=pltpu.CompilerParams(
            dimension_semantics=("arbitrary",), vmem_limit_bytes=VMEM_LIMIT),
        name="swa_sample",
    )(sinks, x_pad, cache_k, cache_v, *weights)


def _row(v):
    return v.reshape(1, -1).astype(F32)


def _pack_lru_weights(j, lru_w_x, lru_b_x, lru_w_y, lru_b_y, lru_conv_w, lru_conv_b,
                      lru_w_ga, lru_b_ga, lru_w_gi, lru_b_gi, lru_lam, lru_w_out, lru_b_out, g, b):
    wxy = jnp.concatenate([lru_w_x[j], lru_w_y[j]], axis=1).astype(BF16)
    bxy = jnp.concatenate([lru_b_x[j], lru_b_y[j]]).reshape(1, -1)
    wg = jnp.concatenate([lru_w_ga[j], lru_w_gi[j]], axis=2).astype(BF16)
    bg = jnp.concatenate([lru_b_ga[j], lru_b_gi[j]], axis=1)[:, None, :]
    return (wxy, bxy, lru_conv_w[j], _row(lru_conv_b[j]), wg, bg, _row(lru_lam[j]),
            lru_w_out[j].astype(BF16), _row(lru_b_out[j]), _row(g), _row(b))


def _pack_swa_weights(j, swa_w_qkv, swa_b_qkv, swa_w_o, swa_b_o, g, b):
    w, bias = swa_w_qkv[j], swa_b_qkv[j]
    wq = w[:, :Q_DIM].reshape(D_MODEL, N_KV, GROUP, HEAD_DIM).transpose(0, 2, 1, 3).reshape(D_MODEL, Q_DIM)
    bq = bias[:Q_DIM].reshape(N_KV, GROUP, HEAD_DIM).transpose(1, 0, 2).reshape(Q_DIM)
    wqkv = jnp.concatenate([wq, w[:, Q_DIM:]], axis=1).astype(BF16)
    bqkv = jnp.concatenate([bq, bias[Q_DIM:]]).reshape(1, -1)
    wo = swa_w_o[j].reshape(N_KV, GROUP, HEAD_DIM, D_MODEL).transpose(1, 0, 2, 3).reshape(Q_DIM, D_MODEL)
    return (wqkv, bqkv, wo.astype(BF16), _row(swa_b_o[j]), _row(g), _row(b))


def kernel(x_prompt, x_sample, state_lru_h, state_lru_conv, cache_swa_k, cache_swa_v,
           lru_w_x, lru_b_x, lru_w_y, lru_b_y, lru_conv_w, lru_conv_b,
           lru_w_ga, lru_b_ga, lru_w_gi, lru_b_gi, lru_lam, lru_w_out, lru_b_out,
           swa_w_qkv, swa_b_qkv, swa_sinks, swa_w_o, swa_b_o,
           mlp_w_up, mlp_w_down, ln1_g, ln1_b, ln2_g, ln2_b):
    bsz, seq, _ = x_prompt.shape
    nb, nt, _ = x_sample.shape
    wbuf = cache_swa_k.shape[2]

    mlp_w = [(mlp_w_up[i].astype(BF16), mlp_w_down[i].astype(BF16), _row(ln2_g[i]), _row(ln2_b[i]))
             for i in range(DEPTH)]

    def mlp(x2d, i):
        return _mlp_block(x2d, *mlp_w[i])

    lw = _pack_lru_weights(0, lru_w_x, lru_b_x, lru_w_y, lru_b_y, lru_conv_w, lru_conv_b,
                           lru_w_ga, lru_b_ga, lru_w_gi, lru_b_gi, lru_lam, lru_w_out, lru_b_out,
                           ln1_g[0], ln1_b[0])
    def chunk_pack(ax, ay):
        lead = ax.shape[:-1]
        cw2 = 2 * LRU_BLOCK
        return jnp.concatenate([ax.reshape(*lead, -1, cw2), ay.reshape(*lead, -1, cw2)],
                               axis=-1).reshape(*lead, 2 * D_RNN)

    lw_p = (chunk_pack(lru_w_x[0], lru_w_y[0]).astype(BF16),
            chunk_pack(lru_b_x[0], lru_b_y[0]).reshape(1, -1)) + lw[2:]
    xp, h_p, conv_p = _layer0_prompt(x_prompt, lw_p, mlp_w[0])

    xs_tm = x_sample.transpose(1, 0, 2).reshape(nt * nb, D_MODEL)
    cs_tm = state_lru_conv[0].transpose(1, 0, 2).reshape((CONV_W - 1) * nb, D_RNN)
    xs_tm, h_s, conv_s_tm = _lru_sample(xs_tm, state_lru_h[0], cs_tm, lw, nb, nt)
    xs_tm = mlp(xs_tm, 0)
    conv_s = conv_s_tm.reshape(CONV_W - 1, nb, D_RNN).transpose(1, 0, 2)

    sw = _pack_swa_weights(0, swa_w_qkv, swa_b_qkv, swa_w_o, swa_b_o, ln1_g[1], ln1_b[1])
    sinks = swa_sinks[0].astype(F32)
    xp, k_p, v_p = _swa_prompt(xp.reshape(bsz, seq, D_MODEL), sinks, sw)
    xp = mlp(xp.reshape(bsz * seq, D_MODEL), 1).reshape(bsz, seq, D_MODEL)

    xs_bm = xs_tm.reshape(nt, nb, D_MODEL).transpose(1, 0, 2)
    xs_pad = jnp.pad(xs_bm, ((0, 0), (0, SMP_PAD_T - nt), (0, 0))).reshape(nb * SMP_PAD_T, D_MODEL)
    ck = cache_swa_k[0].reshape(nb, wbuf, KV_DIM)
    cv = cache_swa_v[0].reshape(nb, wbuf, KV_DIM)
    ys_pad, k_new, v_new = _swa_sample(xs_pad, ck, cv, sinks, sw, nt)
    xs = ys_pad.reshape(nb, SMP_PAD_T, D_MODEL)[:, :nt].reshape(nb * nt, D_MODEL)
    xs = mlp(xs, 1).reshape(nb, nt, D_MODEL)
    k_new = k_new.reshape(nb, SMP_PAD_T, KV_DIM)[:, :nt]
    v_new = v_new.reshape(nb, SMP_PAD_T, KV_DIM)[:, :nt]
    k_s = jnp.concatenate([ck, k_new], axis=1)[:, nt:]
    v_s = jnp.concatenate([cv, v_new], axis=1)[:, nt:]

    kv_shape = (1, -1, min(WINDOW, seq), N_KV, HEAD_DIM)
    return (xp, xs,
            h_p.reshape(1, bsz, D_RNN), conv_p[None],
            k_p.reshape(kv_shape), v_p.reshape(kv_shape),
            h_s[None], conv_s[None],
            k_s.reshape(1, nb, wbuf, N_KV, HEAD_DIM), v_s.reshape(1, nb, wbuf, N_KV, HEAD_DIM))
```

```python
import functools

import jax
import jax.numpy as jnp
from jax import lax
from jax.experimental import pallas as pl
from jax.experimental.pallas import tpu as pltpu

F32 = jnp.float32
BF16 = jnp.bfloat16

D_MODEL = 1024
D_RNN = 1024
D_FF = 4096
LRU_BLOCKS = 8
LRU_BLOCK = 128
CONV_W = 4
LRU_C = 8.0
N_HEADS = 16
HEAD_DIM = 64
N_KV = 2
GROUP = 8
WINDOW = 128
KV_DIM = N_KV * HEAD_DIM
Q_DIM = N_HEADS * HEAD_DIM
QKV_DIM = Q_DIM + 2 * KV_DIM
DEPTH = 2
ALPHA = (2.0 * DEPTH) ** 0.25
LN_EPS = 1e-5

SUBLANES = 8
VMEM_LIMIT = 48 * 1024 * 1024

MLP_TM = 512
MLP_FC = 1024
LRU_TT = 256
SWA_TT = 256
SMP_PAD_T = 8
SMP_BB = 16


def _const_spec(shape):
    nd = len(shape)
    return pl.BlockSpec(shape, lambda *_: (0,) * nd, pipeline_mode=pl.Buffered(1))


def _dot(a, b):
    return jnp.dot(a, b, preferred_element_type=F32)


def _dot_nt(a, b):
    return lax.dot_general(a, b, (((1,), (1,)), ((), ())), preferred_element_type=F32)


def _layer_norm(z, g, b):
    mu = jnp.mean(z, axis=-1, keepdims=True)
    d = z - mu
    var = jnp.mean(d * d, axis=-1, keepdims=True)
    return d * lax.rsqrt(var + LN_EPS) * g + b


def _sigmoid(x):
    return 0.5 * jnp.tanh(0.5 * x) + 0.5


def _mlp_ln(x, wup_ref, wdn_ref, g_ref, b_ref):
    xb = x.astype(BF16)
    acc = jnp.zeros(x.shape, F32)
    for j in range(D_FF // MLP_FC):
        h = _dot(xb, wup_ref[:, j * MLP_FC:(j + 1) * MLP_FC])
        h = jnp.square(jnp.maximum(h, 0.0)).astype(BF16)
        acc = acc + _dot(h, wdn_ref[j * MLP_FC:(j + 1) * MLP_FC, :])
    return _layer_norm(ALPHA * x + acc, g_ref[...], b_ref[...])


def _mlp_kernel(x_ref, wup_ref, wdn_ref, g_ref, b_ref, o_ref):
    o_ref[...] = _mlp_ln(x_ref[...], wup_ref, wdn_ref, g_ref, b_ref)


def _mlp_weight_specs():
    return [_const_spec((D_MODEL, D_FF)), _const_spec((D_FF, D_MODEL)),
            _const_spec((1, D_MODEL)), _const_spec((1, D_MODEL))]


def _mlp_block(x, w_up, w_dn, g, b):
    n = x.shape[0]
    tm = min(MLP_TM, n)
    assert n % tm == 0
    row_spec = pl.BlockSpec((tm, D_MODEL), lambda i: (i, 0))
    return pl.pallas_call(
        _mlp_kernel,
        grid=(n // tm,),
        in_specs=[row_spec] + _mlp_weight_specs(),
        out_specs=row_spec,
        out_shape=jax.ShapeDtypeStruct((n, D_MODEL), F32),
        compiler_params=pltpu.CompilerParams(
            dimension_semantics=("arbitrary",), vmem_limit_bytes=VMEM_LIMIT),
        name="mlp_ln",
    )(x, w_up, w_dn, g, b)


_GELU_C0 = (2.0 / jnp.pi) ** 0.5
_GELU_C1 = _GELU_C0 * 0.044715


def _gelu_tanh(x):
    t = jnp.tanh(x * (_GELU_C0 + _GELU_C1 * (x * x)))
    hx = 0.5 * x
    return hx + hx * t


def _lru_in_proj(x, wxy_ref, bxy_ref):
    uy = _dot(x.astype(BF16), wxy_ref[...]) + bxy_ref[...]
    u = uy[:, :D_RNN]
    y = _gelu_tanh(uy[:, D_RNN:])
    return u, y


def _neg_c_softplus_neg(lam):
    z = -lam
    sp = jnp.maximum(z, 0.0) + jnp.log1p(jnp.exp(-jnp.abs(z)))
    return -LRU_C * sp


def _lru_gate_chunk(xc_n, n, wg_ref, bg_ref, nla):
    return _lru_gate_math(_dot(xc_n.astype(BF16), wg_ref[n]), xc_n, n, bg_ref, nla)


def _lru_gate_math(gt, xc_n, n, bg_ref, nla):
    lo = n * LRU_BLOCK
    gt = gt + bg_ref[n]
    r = _sigmoid(gt[:, :LRU_BLOCK])
    gi = _sigmoid(gt[:, LRU_BLOCK:])
    log_a = r * nla[:, lo:lo + LRU_BLOCK]
    a = jnp.exp(log_a)
    v = -jnp.tanh(log_a) * (a * a + 1.0)
    mult = jnp.where(v > 0.0, v * lax.rsqrt(v), 0.0)
    return a, mult * (gi * xc_n)


def _lru_out(h, y, x, wo_ref, bo_ref, g_ref, b_ref):
    out = _dot((h * y).astype(BF16), wo_ref[...]) + bo_ref[...]
    return _layer_norm(ALPHA * x + out, g_ref[...], b_ref[...])


def _interleave_matrix(tt):
    r = jnp.arange(tt)
    src_time = (r % SUBLANES) * (tt // SUBLANES) + r // SUBLANES
    return (src_time[:, None] == jnp.arange(tt)[None, :]).astype(BF16)


def _shift_one_step(z, carry_row):
    tt = z.shape[0]
    last = z[tt - SUBLANES:, :]
    row = lax.broadcasted_iota(jnp.int32, last.shape, 0)
    first = jnp.where(row == 0, carry_row, pltpu.roll(last, 1, axis=0))
    return jnp.concatenate([first, z[:tt - SUBLANES, :]], axis=0)


def _scan_interleaved(a, b, h_in):
    n_steps = a.shape[0] // SUBLANES
    h = b[0:SUBLANES, :]
    p = a[0:SUBLANES, :]
    hs, ps = [h], [p]
    for i in range(1, n_steps):
        ai = a[i * SUBLANES:(i + 1) * SUBLANES, :]
        h = ai * h + b[i * SUBLANES:(i + 1) * SUBLANES, :]
        p = ai * p
        hs.append(h)
        ps.append(p)
    row = lax.broadcasted_iota(jnp.int32, h.shape, 0)
    cin = h_in
    cvec = jnp.zeros(h.shape, F32)
    for c in range(SUBLANES):
        cvec = jnp.where(row == c, cin, cvec)
        cin = h[c:c + 1, :] + p[c:c + 1, :] * cin
    full = jnp.concatenate([hs[i] + ps[i] * cvec for i in range(n_steps)], axis=0)
    return full, cin


def _layer0_prompt_kernel(x_ref, pm_ref, pmt_ref, wxy_ref, bxy_ref, cw_ref, cb_ref, wg_ref, bg_ref, lam_ref,
                          wo_ref, bo_ref, g1_ref, b1_ref, wup_ref, wdn_ref, g2_ref, b2_ref,
                          y_ref, hout_ref, cout_ref,
                          hy_scr, hc_scr, cc_scr, ul_scr, x1_scr, *, tiles_per_seq):
    tt = LRU_TT
    s = pl.program_id(0)
    n_tiles = pl.num_programs(0) - 1
    t = s % tiles_per_seq
    slot = s % 2

    @pl.when(s == 0)
    def _():
        x1_scr[1] = jnp.zeros((tt, D_MODEL), F32)

    @pl.when(t == 0)
    def _():
        hc_scr[...] = jnp.zeros(hc_scr.shape, F32)
        cc_scr[...] = jnp.zeros(cc_scr.shape, F32)

    x1 = x1_scr[1 - slot]
    x1b = x1.astype(BF16)
    acc = jnp.zeros((tt, D_MODEL), F32)
    x = x_ref[...]
    xb = x.astype(BF16)
    nla = _neg_c_softplus_neg(lam_ref[...])
    n_mlp = LRU_BLOCKS
    fc = D_FF // n_mlp
    cw2 = 2 * LRU_BLOCK
    n_col = D_RNN // cw2

    def mlp_up(n):
        h = _dot(x1b, wup_ref[:, n * fc:(n + 1) * fc])
        return jnp.square(jnp.maximum(h, 0.0)).astype(BF16)

    def mlp_down(acc, h, n):
        return acc + _dot(h, wdn_ref[n * fc:(n + 1) * fc, :])

    def in_proj(j):
        return _dot(xpb, wxy_ref[:, 2 * j * cw2:2 * (j + 1) * cw2]) + bxy_ref[:, 2 * j * cw2:2 * (j + 1) * cw2]

    xp = _dot(pm_ref[...], xb)
    hs = {0: mlp_up(0)}
    xpb = xp.astype(BF16)
    uy = in_proj(0)
    for j in range(n_col):
        c0 = j * cw2
        hs[j + 1] = mlp_up(j + 1)
        acc = mlp_down(acc, hs.pop(j), j)
        u = uy[:, :cw2]
        y = _gelu_tanh(uy[:, cw2:])
        z = cw_ref[0:1, c0:c0 + cw2] * u
        for k in range(1, CONV_W):
            carry = cc_scr[k - 1:k, c0:c0 + cw2]
            cc_scr[k - 1:k, c0:c0 + cw2] = z[tt - 1:tt, :]
            z = cw_ref[k:k + 1, c0:c0 + cw2] * u + _shift_one_step(z, carry)
        xc = z + cb_ref[:, c0:c0 + cw2]
        ul_scr[:, c0:c0 + cw2] = u[tt - (CONV_W - 1) * SUBLANES:, :]
        xcs = [xc[:, q * LRU_BLOCK:(q + 1) * LRU_BLOCK] for q in range(2)]
        gts = [_dot(xcs[q].astype(BF16), wg_ref[2 * j + q]) for q in range(2)]
        if j + 1 < n_col:
            uy = in_proj(j + 1)
        for q in range(2):
            n = 2 * j + q
            lo = n * LRU_BLOCK
            a, b = _lru_gate_math(gts[q], xcs[q], n, bg_ref, nla)
            h, h_end = _scan_interleaved(a, b, hc_scr[0:1, lo:lo + LRU_BLOCK])
            hc_scr[0:1, lo:lo + LRU_BLOCK] = h_end
            hy_scr[:, lo:lo + LRU_BLOCK] = (h * y[:, q * LRU_BLOCK:(q + 1) * LRU_BLOCK]).astype(BF16)

    nxt = n_col + 1
    hs[nxt] = mlp_up(nxt)
    acc = mlp_down(acc, hs.pop(n_col), n_col)
    hs[nxt + 1] = mlp_up(nxt + 1)
    hy = _dot(pmt_ref[...], hy_scr[...])
    acc = mlp_down(acc, hs.pop(nxt), nxt)
    hs[nxt + 2] = mlp_up(nxt + 2)
    out = _dot(hy.astype(BF16), wo_ref[...]) + bo_ref[...]
    acc = mlp_down(acc, hs.pop(nxt + 1), nxt + 1)
    acc = mlp_down(acc, hs.pop(nxt + 2), nxt + 2)
    assert not hs and nxt + 2 == n_mlp - 1
    x1_scr[slot] = _layer_norm(ALPHA * x + out, g1_ref[...], b1_ref[...])
    y_ref[...] = _layer_norm(ALPHA * x1 + acc, g2_ref[...], b2_ref[...])

    @pl.when(jnp.logical_and(t == tiles_per_seq - 1, s < n_tiles))
    def _():
        hout_ref[...] = hc_scr[0:1, :]
        for k in range(CONV_W - 1):
            r = (k + 1) * SUBLANES - 1
            cout_ref[k:k + 1, :] = ul_scr[r:r + 1, :]


def _lru_sample_kernel(x_ref, h0_ref, cs_ref, wxy_ref, bxy_ref, cw_ref, cb_ref, wg_ref, bg_ref,
                       lam_ref, wo_ref, bo_ref, g_ref, b_ref,
                       y_ref, hout_ref, cout_ref, *, nb, nt):
    x = x_ref[...]
    u, y = _lru_in_proj(x, wxy_ref, bxy_ref)
    upad = jnp.concatenate([cs_ref[...], u], axis=0)
    xc = cb_ref[...]
    for k in range(CONV_W):
        xc = xc + cw_ref[k:k + 1, :] * upad[k * nb:(k + nt) * nb, :]
    nla = _neg_c_softplus_neg(lam_ref[...])
    a_parts, b_parts = [], []
    for n in range(LRU_BLOCKS):
        lo = n * LRU_BLOCK
        a, b = _lru_gate_chunk(xc[:, lo:lo + LRU_BLOCK], n, wg_ref, bg_ref, nla)
        a_parts.append(a)
        b_parts.append(b)
    a = jnp.concatenate(a_parts, axis=1)
    b = jnp.concatenate(b_parts, axis=1)
    h = h0_ref[...]
    hs = []
    for t in range(nt):
        h = a[t * nb:(t + 1) * nb, :] * h + b[t * nb:(t + 1) * nb, :]
        hs.append(h)
    y_ref[...] = _lru_out(jnp.concatenate(hs, axis=0), y, x, wo_ref, bo_ref, g_ref, b_ref)
    hout_ref[...] = h
    cout_ref[...] = upad[nt * nb:, :]


def _lru_weight_specs():
    return [_const_spec((D_MODEL, 2 * D_RNN)), _const_spec((1, 2 * D_RNN)),
            _const_spec((CONV_W, D_RNN)), _const_spec((1, D_RNN)),
            _const_spec((LRU_BLOCKS, LRU_BLOCK, 2 * LRU_BLOCK)), _const_spec((LRU_BLOCKS, 1, 2 * LRU_BLOCK)),
            _const_spec((1, D_RNN)), _const_spec((D_RNN, D_MODEL)), _const_spec((1, D_MODEL)),
            _const_spec((1, D_MODEL)), _const_spec((1, D_MODEL))]


def _layer0_prompt(x, lru_weights, mlp_weights):
    bsz, seq, _ = x.shape
    tt = LRU_TT
    assert seq % tt == 0
    tps = seq // tt
    n_tiles = bsz * tps

    def lru_tile(s):
        s = jnp.minimum(s, n_tiles - 1)
        return (s // tps, s % tps, 0)

    def mlp_tile(s):
        s = jnp.maximum(s - 1, 0)
        return (s // tps, s % tps, 0)

    def state_block(s):
        return (jnp.minimum(s, n_tiles - 1) // tps, 0, 0)

    pm = _interleave_matrix(tt)
    return pl.pallas_call(
        functools.partial(_layer0_prompt_kernel, tiles_per_seq=tps),
        grid=(n_tiles + 1,),
        in_specs=[pl.BlockSpec((None, tt, D_MODEL), lru_tile), _const_spec((tt, tt)), _const_spec((tt, tt))]
        + _lru_weight_specs() + _mlp_weight_specs(),
        out_specs=[pl.BlockSpec((None, tt, D_MODEL), mlp_tile),
                   pl.BlockSpec((None, 1, D_RNN), state_block),
                   pl.BlockSpec((None, CONV_W - 1, D_RNN), state_block)],
        out_shape=[jax.ShapeDtypeStruct((bsz, seq, D_MODEL), F32),
                   jax.ShapeDtypeStruct((bsz, 1, D_RNN), F32),
                   jax.ShapeDtypeStruct((bsz, CONV_W - 1, D_RNN), F32)],
        scratch_shapes=[pltpu.VMEM((tt, D_RNN), BF16),
                        pltpu.VMEM((SUBLANES, D_RNN), F32),
                        pltpu.VMEM((SUBLANES, D_RNN), F32),
                        pltpu.VMEM(((CONV_W - 1) * SUBLANES, D_RNN), F32),
                        pltpu.VMEM((2, tt, D_MODEL), F32)],
        compiler_params=pltpu.CompilerParams(
            dimension_semantics=("arbitrary",), vmem_limit_bytes=VMEM_LIMIT),
        name="layer0_prompt",
    )(x, pm, pm.T, *lru_weights, *mlp_weights)


def _lru_sample(x_tm, h0, cs_tm, weights, nb, nt):
    rows = nb * nt
    return pl.pallas_call(
        functools.partial(_lru_sample_kernel, nb=nb, nt=nt),
        grid=(1,),
        in_specs=[_const_spec((rows, D_MODEL)), _const_spec((nb, D_RNN)),
                  _const_spec(((CONV_W - 1) * nb, D_RNN))] + _lru_weight_specs(),
        out_specs=[_const_spec((rows, D_MODEL)), _const_spec((nb, D_RNN)),
                   _const_spec(((CONV_W - 1) * nb, D_RNN))],
        out_shape=[jax.ShapeDtypeStruct((rows, D_MODEL), F32),
                   jax.ShapeDtypeStruct((nb, D_RNN), F32),
                   jax.ShapeDtypeStruct(((CONV_W - 1) * nb, D_RNN), F32)],
        compiler_params=pltpu.CompilerParams(
            dimension_semantics=("arbitrary",), vmem_limit_bytes=VMEM_LIMIT),
        name="lru_sample",
    )(x_tm, h0, cs_tm, *weights)


def _lane_is_kv0(shape):
    return lax.broadcasted_iota(jnp.int32, shape, len(shape) - 1) < HEAD_DIM


def _sink_softmax_pv(s, mask, sink, vb):
    s = jnp.where(mask, s, -jnp.inf)
    m = jnp.maximum(jnp.max(s, axis=-1, keepdims=True), sink)
    p = jnp.exp(s - m)
    denom = jnp.sum(p, axis=-1, keepdims=True) + jnp.exp(sink - m)
    return _dot(p.astype(BF16), vb) / denom


def _sink_softmax(s, mask, sink):
    s = jnp.where(mask, s, -jnp.inf)
    m = jnp.maximum(jnp.max(s, axis=-1, keepdims=True), sink)
    p = jnp.exp(s - m)
    denom = jnp.sum(p, axis=-1, keepdims=True) + jnp.exp(sink - m)
    return p.astype(BF16), 1.0 / denom


def _layer1_prompt_kernel(sink_ref, x_ref, wqkv_ref, bqkv_ref, wo_ref, bo_ref, g1_ref, b1_ref,
                          wup_ref, wdn_ref, g2_ref, b2_ref,
                          y_ref, kout_ref, vout_ref,
                          q_scr, k0_scr, k1_scr, v_scr, o_scr, kl_scr, vl_scr, x1_scr, *, tiles_per_seq):
    tt = SWA_TT
    blk = WINDOW
    s_id = pl.program_id(0)
    n_tiles = pl.num_programs(0) - 1
    t = s_id % tiles_per_seq
    slot = s_id % 2

    @pl.when(s_id == 0)
    def _():
        x1_scr[1] = jnp.zeros((tt, D_MODEL), F32)

    @pl.when(t == 0)
    def _():
        zeros = jnp.zeros((blk, KV_DIM), BF16)
        k0_scr[0:blk, :] = zeros
        k1_scr[0:blk, :] = zeros
        v_scr[0:blk, :] = zeros

    x1 = x1_scr[1 - slot]
    x1b = x1.astype(BF16)
    acc = jnp.zeros((tt, D_MODEL), F32)
    x = x_ref[...]
    xb = x.astype(BF16)
    n_mlp = 8
    fc = D_FF // n_mlp
    hs = {}
    nxt_up = [0]
    nxt_dn = [0]

    def mlp_up():
        n = nxt_up[0]
        nxt_up[0] += 1
        h = _dot(x1b, wup_ref[:, n * fc:(n + 1) * fc])
        hs[n] = jnp.square(jnp.maximum(h, 0.0)).astype(BF16)

    def mlp_down(acc):
        n = nxt_dn[0]
        nxt_dn[0] += 1
        return acc + _dot(hs.pop(n), wdn_ref[n * fc:(n + 1) * fc, :])

    mlp_up()
    qkv = _dot(xb, wqkv_ref[...]) + bqkv_ref[...]
    mlp_up()
    acc = mlp_down(acc)
    k = qkv[:, Q_DIM:Q_DIM + KV_DIM]
    v = qkv[:, Q_DIM + KV_DIM:]
    q_scr[...] = (qkv[:, :Q_DIM] * (HEAD_DIM ** -0.5)).astype(BF16)
    kv0 = _lane_is_kv0((tt, KV_DIM))
    k0_scr[blk:, :] = jnp.where(kv0, k, 0.0).astype(BF16)
    k1_scr[blk:, :] = jnp.where(kv0, 0.0, k).astype(BF16)
    v_scr[blk:, :] = v.astype(BF16)
    kl_scr[...] = k[tt - blk:, :]
    vl_scr[...] = v[tt - blk:, :]

    qi = lax.broadcasted_iota(jnp.int32, (blk, 2 * blk), 0)
    kj = lax.broadcasted_iota(jnp.int32, (blk, 2 * blk), 1)
    band = (kj > qi) & (kj <= qi + blk)
    kv0_o = _lane_is_kv0((blk, KV_DIM))

    for i in range(tt // blk):
        first = jnp.logical_and(t == 0, i == 0)
        mask = band & (kj >= jnp.where(first, blk, 0))
        r0 = i * blk
        qs = jnp.concatenate(
            [q_scr[r0:r0 + blk, g * KV_DIM:(g + 1) * KV_DIM] for g in range(GROUP)], axis=0)
        vb = v_scr[r0:r0 + 2 * blk, :]
        scores = [_dot_nt(qs, k_scr[r0:r0 + 2 * blk, :]) for k_scr in (k0_scr, k1_scr)]
        mlp_up()
        acc = mlp_down(acc)
        outs = []
        for kh in range(N_KV):
            ps, rs = [], []
            for g in range(GROUP):
                p, r = _sink_softmax(scores[kh][g * blk:(g + 1) * blk, :], mask, sink_ref[kh * GROUP + g])
                ps.append(p)
                rs.append(r)
            o = _dot(jnp.concatenate(ps, axis=0), vb)
            outs.append([o[g * blk:(g + 1) * blk, :] * rs[g] for g in range(GROUP)])
        for g in range(GROUP):
            o_scr[r0:r0 + blk, g * KV_DIM:(g + 1) * KV_DIM] = jnp.where(
                kv0_o, outs[0][g], outs[1][g]).astype(BF16)

    mlp_up()
    acc = mlp_down(acc)
    out = _dot(o_scr[...], wo_ref[...]) + bo_ref[...]
    while nxt_up[0] < n_mlp:
        mlp_up()
        acc = mlp_down(acc)
    acc = mlp_down(acc)
    assert not hs and nxt_dn[0] == n_mlp
    x1_scr[slot] = _layer_norm(ALPHA * x + out, g1_ref[...], b1_ref[...])
    y_ref[...] = _layer_norm(ALPHA * x1 + acc, g2_ref[...], b2_ref[...])

    k0_scr[0:blk, :] = k0_scr[tt:tt + blk, :]
    k1_scr[0:blk, :] = k1_scr[tt:tt + blk, :]
    v_scr[0:blk, :] = v_scr[tt:tt + blk, :]

    @pl.when(jnp.logical_and(t == tiles_per_seq - 1, s_id < n_tiles))
    def _():
        kout_ref[...] = kl_scr[...]
        vout_ref[...] = vl_scr[...]


def _swa_weight_specs():
    return [_const_spec((D_MODEL, QKV_DIM)), _const_spec((1, QKV_DIM)),
            _const_spec((Q_DIM, D_MODEL)), _const_spec((1, D_MODEL)),
            _const_spec((1, D_MODEL)), _const_spec((1, D_MODEL))]


def _layer1_prompt(x, sinks, swa_weights, mlp_weights):
    bsz, seq, _ = x.shape
    tt = SWA_TT
    assert seq % tt == 0 and seq >= WINDOW and tt % WINDOW == 0
    tps = seq // tt
    n_tiles = bsz * tps

    def swa_tile(s):
        s = jnp.minimum(s, n_tiles - 1)
        return (s // tps, s % tps, 0)

    def mlp_tile(s):
        s = jnp.maximum(s - 1, 0)
        return (s // tps, s % tps, 0)

    kv_spec = pl.BlockSpec((None, WINDOW, KV_DIM), lambda s: (jnp.minimum(s, n_tiles - 1) // tps, 0, 0))
    return pl.pallas_call(
        functools.partial(_layer1_prompt_kernel, tiles_per_seq=tps),
        grid=(n_tiles + 1,),
        in_specs=[pl.BlockSpec(memory_space=pltpu.SMEM), pl.BlockSpec((None, tt, D_MODEL), swa_tile)]
        + _swa_weight_specs() + _mlp_weight_specs(),
        out_specs=[pl.BlockSpec((None, tt, D_MODEL), mlp_tile), kv_spec, kv_spec],
        out_shape=[jax.ShapeDtypeStruct((bsz, seq, D_MODEL), F32),
                   jax.ShapeDtypeStruct((bsz, WINDOW, KV_DIM), F32),
                   jax.ShapeDtypeStruct((bsz, WINDOW, KV_DIM), F32)],
        scratch_shapes=[pltpu.VMEM((tt, Q_DIM), BF16),
                        pltpu.VMEM((tt + WINDOW, KV_DIM), BF16),
                        pltpu.VMEM((tt + WINDOW, KV_DIM), BF16),
                        pltpu.VMEM((tt + WINDOW, KV_DIM), BF16),
                        pltpu.VMEM((tt, Q_DIM), BF16),
                        pltpu.VMEM((WINDOW, KV_DIM), F32),
                        pltpu.VMEM((WINDOW, KV_DIM), F32),
                        pltpu.VMEM((2, tt, D_MODEL), F32)],
        compiler_params=pltpu.CompilerParams(
            dimension_semantics=("arbitrary",), vmem_limit_bytes=VMEM_LIMIT),
        name="layer1_prompt",
    )(sinks, x, *swa_weights, *mlp_weights)


def _swa_sample_kernel(sink_ref, x_ref, ck_ref, cv_ref, wqkv_ref, bqkv_ref, wo_ref, bo_ref, g_ref, b_ref,
                       y_ref, knew_ref, vnew_ref,
                       q_scr, kb_scr, vb_scr, o_scr, *, nt):
    pt = SMP_PAD_T
    wbuf = ck_ref.shape[1]
    nkeys = kb_scr.shape[0]
    step = pl.program_id(0)

    @pl.when(step == 0)
    def _():
        qkv = _dot(x_ref[...].astype(BF16), wqkv_ref[...]) + bqkv_ref[...]
        q_scr[...] = qkv[:, :Q_DIM] * (HEAD_DIM ** -0.5)
        knew_ref[...] = qkv[:, Q_DIM:Q_DIM + KV_DIM]
        vnew_ref[...] = qkv[:, Q_DIM + KV_DIM:]
        kb_scr[...] = jnp.zeros(kb_scr.shape, F32)
        vb_scr[...] = jnp.zeros(vb_scr.shape, F32)

    rows = GROUP * pt
    ri = lax.broadcasted_iota(jnp.int32, (rows, nkeys), 0)
    kj = lax.broadcasted_iota(jnp.int32, (rows, nkeys), 1)
    tq = ri % pt
    in_buf = kj < wbuf
    mask = (in_buf & (kj - wbuf + WINDOW > tq)) | (
        jnp.logical_not(in_buf) & (kj - wbuf <= tq) & (kj - wbuf < nt))
    gi = lax.broadcasted_iota(jnp.int32, (rows, 1), 0) // pt
    sink_cols = []
    for kh in range(N_KV):
        col = jnp.zeros((rows, 1), F32)
        for g in range(GROUP):
            col = jnp.where(gi == g, sink_ref[kh * GROUP + g], col)
        sink_cols.append(col)
    kv0_k = _lane_is_kv0((nkeys, KV_DIM))
    kv0_o = _lane_is_kv0((rows, KV_DIM))

    def per_batch(b, carry):
        r0 = pl.multiple_of((step * SMP_BB + b) * pt, pt)
        kb_scr[0:wbuf, :] = ck_ref[b]
        vb_scr[0:wbuf, :] = cv_ref[b]
        kb_scr[wbuf:wbuf + pt, :] = knew_ref[pl.ds(r0, pt), :]
        vb_scr[wbuf:wbuf + pt, :] = vnew_ref[pl.ds(r0, pt), :]
        qs = jnp.concatenate(
            [q_scr[pl.ds(r0, pt), g * KV_DIM:(g + 1) * KV_DIM] for g in range(GROUP)], axis=0)
        qs = qs.astype(BF16)
        kb = kb_scr[...]
        vb = vb_scr[...].astype(BF16)
        o0 = _sink_softmax_pv(_dot_nt(qs, jnp.where(kv0_k, kb, 0.0).astype(BF16)), mask, sink_cols[0], vb)
        o1 = _sink_softmax_pv(_dot_nt(qs, jnp.where(kv0_k, 0.0, kb).astype(BF16)), mask, sink_cols[1], vb)
        o = jnp.where(kv0_o, o0, o1)
        for g in range(GROUP):
            o_scr[pl.ds(r0, pt), g * KV_DIM:(g + 1) * KV_DIM] = o[g * pt:(g + 1) * pt, :]
        return carry

    lax.fori_loop(0, SMP_BB, per_batch, 0)

    @pl.when(step == pl.num_programs(0) - 1)
    def _():
        out = _dot(o_scr[...].astype(BF16), wo_ref[...]) + bo_ref[...]
        y_ref[...] = _layer_norm(ALPHA * x_ref[...] + out, g_ref[...], b_ref[...])


def _swa_sample(x_pad, cache_k, cache_v, sinks, weights, nt):
    rows = x_pad.shape[0]
    nb, wbuf, _ = cache_k.shape
    assert nb % SMP_BB == 0 and rows == nb * SMP_PAD_T and nt <= SMP_PAD_T
    nkeys = 2 * WINDOW
    assert wbuf + SMP_PAD_T <= nkeys
    cache_spec = pl.BlockSpec((SMP_BB, wbuf, KV_DIM), lambda i: (i, 0, 0))
    return pl.pallas_call(
        functools.partial(_swa_sample_kernel, nt=nt),
        grid=(nb // SMP_BB,),
        in_specs=[pl.BlockSpec(memory_space=pltpu.SMEM), _const_spec((rows, D_MODEL)),
                  cache_spec, cache_spec] + _swa_weight_specs(),
        out_specs=[_const_spec((rows, D_MODEL)), _const_spec((rows, KV_DIM)), _const_spec((rows, KV_DIM))],
        out_shape=[jax.ShapeDtypeStruct((rows, D_MODEL), F32),
                   jax.ShapeDtypeStruct((rows, KV_DIM), F32),
                   jax.ShapeDtypeStruct((rows, KV_DIM), F32)],
        scratch_shapes=[pltpu.VMEM((rows, Q_DIM), F32),
                        pltpu.VMEM((nkeys, KV_DIM), F32),
                        pltpu.VMEM((nkeys, KV_DIM), F32),
                        pltpu.VMEM((rows, Q_DIM), F32)],
        compiler_params=pltpu.CompilerParams(
            dimension_semantics=("arbitrary",), vmem_limit_bytes=VMEM_LIMIT),
        name="swa_sample",
    )(sinks, x_pad, cache_k, cache_v, *weights)


def _row(v):
    return v.reshape(1, -1).astype(F32)


def _pack_lru_weights(j, lru_w_x, lru_b_x, lru_w_y, lru_b_y, lru_conv_w, lru_conv_b,
                      lru_w_ga, lru_b_ga, lru_w_gi, lru_b_gi, lru_lam, lru_w_out, lru_b_out, g, b):
    wxy = jnp.concatenate([lru_w_x[j], lru_w_y[j]], axis=1).astype(BF16)
    bxy = jnp.concatenate([lru_b_x[j], lru_b_y[j]]).reshape(1, -1)
    wg = jnp.concatenate([lru_w_ga[j], lru_w_gi[j]], axis=2).astype(BF16)
    bg = jnp.concatenate([lru_b_ga[j], lru_b_gi[j]], axis=1)[:, None, :]
    return (wxy, bxy, lru_conv_w[j], _row(lru_conv_b[j]), wg, bg, _row(lru_lam[j]),
            lru_w_out[j].astype(BF16), _row(lru_b_out[j]), _row(g), _row(b))


def _pack_swa_weights(j, swa_w_qkv, swa_b_qkv, swa_w_o, swa_b_o, g, b):
    w, bias = swa_w_qkv[j], swa_b_qkv[j]
    wq = w[:, :Q_DIM].reshape(D_MODEL, N_KV, GROUP, HEAD_DIM).transpose(0, 2, 1, 3).reshape(D_MODEL, Q_DIM)
    bq = bias[:Q_DIM].reshape(N_KV, GROUP, HEAD_DIM).transpose(1, 0, 2).reshape(Q_DIM)
    wqkv = jnp.concatenate([wq, w[:, Q_DIM:]], axis=1).astype(BF16)
    bqkv = jnp.concatenate([bq, bias[Q_DIM:]]).reshape(1, -1)
    wo = swa_w_o[j].reshape(N_KV, GROUP, HEAD_DIM, D_MODEL).transpose(1, 0, 2, 3).reshape(Q_DIM, D_MODEL)
    return (wqkv, bqkv, wo.astype(BF16), _row(swa_b_o[j]), _row(g), _row(b))


def kernel(x_prompt, x_sample, state_lru_h, state_lru_conv, cache_swa_k, cache_swa_v,
           lru_w_x, lru_b_x, lru_w_y, lru_b_y, lru_conv_w, lru_conv_b,
           lru_w_ga, lru_b_ga, lru_w_gi, lru_b_gi, lru_lam, lru_w_out, lru_b_out,
           swa_w_qkv, swa_b_qkv, swa_sinks, swa_w_o, swa_b_o,
           mlp_w_up, mlp_w_down, ln1_g, ln1_b, ln2_g, ln2_b):
    bsz, seq, _ = x_prompt.shape
    nb, nt, _ = x_sample.shape
    wbuf = cache_swa_k.shape[2]

    mlp_w = [(mlp_w_up[i].astype(BF16), mlp_w_down[i].astype(BF16), _row(ln2_g[i]), _row(ln2_b[i]))
             for i in range(DEPTH)]

    def mlp(x2d, i):
        return _mlp_block(x2d, *mlp_w[i])

    lw = _pack_lru_weights(0, lru_w_x, lru_b_x, lru_w_y, lru_b_y, lru_conv_w, lru_conv_b,
                           lru_w_ga, lru_b_ga, lru_w_gi, lru_b_gi, lru_lam, lru_w_out, lru_b_out,
                           ln1_g[0], ln1_b[0])
    def chunk_pack(ax, ay):
        lead = ax.shape[:-1]
        cw2 = 2 * LRU_BLOCK
        return jnp.concatenate([ax.reshape(*lead, -1, cw2), ay.reshape(*lead, -1, cw2)],
                               axis=-1).reshape(*lead, 2 * D_RNN)

    lw_p = (chunk_pack(lru_w_x[0], lru_w_y[0]).astype(BF16),
            chunk_pack(lru_b_x[0], lru_b_y[0]).reshape(1, -1)) + lw[2:]
    xp, h_p, conv_p = _layer0_prompt(x_prompt, lw_p, mlp_w[0])

    xs_tm = x_sample.transpose(1, 0, 2).reshape(nt * nb, D_MODEL)
    cs_tm = state_lru_conv[0].transpose(1, 0, 2).reshape((CONV_W - 1) * nb, D_RNN)
    xs_tm, h_s, conv_s_tm = _lru_sample(xs_tm, state_lru_h[0], cs_tm, lw, nb, nt)
    xs_tm = mlp(xs_tm, 0)
    conv_s = conv_s_tm.reshape(CONV_W - 1, nb, D_RNN).transpose(1, 0, 2)

    sw = _pack_swa_weights(0, swa_w_qkv, swa_b_qkv, swa_w_o, swa_b_o, ln1_g[1], ln1_b[1])
    sinks = swa_sinks[0].astype(F32)
    xp, k_p, v_p = _layer1_prompt(xp, sinks, sw, mlp_w[1])

    xs_bm = xs_tm.reshape(nt, nb, D_MODEL).transpose(1, 0, 2)
    xs_pad = jnp.pad(xs_bm, ((0, 0), (0, SMP_PAD_T - nt), (0, 0))).reshape(nb * SMP_PAD_T, D_MODEL)
    ck = cache_swa_k[0].reshape(nb, wbuf, KV_DIM)
    cv = cache_swa_v[0].reshape(nb, wbuf, KV_DIM)
    ys_pad, k_new, v_new = _swa_sample(xs_pad, ck, cv, sinks, sw, nt)
    xs = ys_pad.reshape(nb, SMP_PAD_T, D_MODEL)[:, :nt].reshape(nb * nt, D_MODEL)
    xs = mlp(xs, 1).reshape(nb, nt, D_MODEL)
    k_new = k_new.reshape(nb, SMP_PAD_T, KV_DIM)[:, :nt]
    v_new = v_new.reshape(nb, SMP_PAD_T, KV_DIM)[:, :nt]
    k_s = jnp.concatenate([ck, k_new], axis=1)[:, nt:]
    v_s = jnp.concatenate([cv, v_new], axis=1)[:, nt:]

    kv_shape = (1, -1, min(WINDOW, seq), N_KV, HEAD_DIM)
    return (xp, xs,
            h_p.reshape(1, bsz, D_RNN), conv_p[None],
            k_p.reshape(kv_shape), v_p.reshape(kv_shape),
            h_s[None], conv_s[None],
            k_s.reshape(1, nb, wbuf, N_KV, HEAD_DIM), v_s.reshape(1, nb, wbuf, N_KV, HEAD_DIM))
```

```python
import functools

import jax
import jax.numpy as jnp
from jax import lax
from jax.experimental import pallas as pl
from jax.experimental.pallas import tpu as pltpu

F32 = jnp.float32
BF16 = jnp.bfloat16

D_MODEL = 1024
D_RNN = 1024
D_FF = 4096
LRU_BLOCKS = 8
LRU_BLOCK = 128
CONV_W = 4
LRU_C = 8.0
N_HEADS = 16
HEAD_DIM = 64
N_KV = 2
GROUP = 8
WINDOW = 128
KV_DIM = N_KV * HEAD_DIM
Q_DIM = N_HEADS * HEAD_DIM
QKV_DIM = Q_DIM + 2 * KV_DIM
DEPTH = 2
ALPHA = (2.0 * DEPTH) ** 0.25
LN_EPS = 1e-5

SUBLANES = 8
VMEM_LIMIT = 56 * 1024 * 1024

MLP_TM = 512
MLP_FC = 1024
LRU_TT = 512
LRU_IL = 256
SWA_TT = 256
L1_MLP_CHUNKS = 8
SMP_PAD_T = 8
SMP_BB = 16
SMP_GROUP = 4


def _const_spec(shape):
    nd = len(shape)
    return pl.BlockSpec(shape, lambda *_: (0,) * nd, pipeline_mode=pl.Buffered(1))


def _dot(a, b):
    return jnp.dot(a, b, preferred_element_type=F32)


def _dot_nt(a, b):
    return lax.dot_general(a, b, (((1,), (1,)), ((), ())), preferred_element_type=F32)


def _layer_norm(z, g, b):
    mu = jnp.mean(z, axis=-1, keepdims=True)
    d = z - mu
    var = jnp.mean(d * d, axis=-1, keepdims=True)
    return d * lax.rsqrt(var + LN_EPS) * g + b


def _sigmoid(x):
    return 0.5 * jnp.tanh(0.5 * x) + 0.5


def _mlp_ln(x, wup_ref, wdn_ref, g_ref, b_ref):
    xb = x.astype(BF16)
    acc = jnp.zeros(x.shape, F32)
    for j in range(D_FF // MLP_FC):
        h = _dot(xb, wup_ref[:, j * MLP_FC:(j + 1) * MLP_FC])
        h = jnp.square(jnp.maximum(h, 0.0)).astype(BF16)
        acc = acc + _dot(h, wdn_ref[j * MLP_FC:(j + 1) * MLP_FC, :])
    return _layer_norm(ALPHA * x + acc, g_ref[...], b_ref[...])


def _mlp_kernel(x_ref, wup_ref, wdn_ref, g_ref, b_ref, o_ref):
    o_ref[...] = _mlp_ln(x_ref[...], wup_ref, wdn_ref, g_ref, b_ref)


def _mlp_weight_specs():
    return [_const_spec((D_MODEL, D_FF)), _const_spec((D_FF, D_MODEL)),
            _const_spec((1, D_MODEL)), _const_spec((1, D_MODEL))]


def _mlp_block(x, w_up, w_dn, g, b):
    n = x.shape[0]
    tm = min(MLP_TM, n)
    assert n % tm == 0
    row_spec = pl.BlockSpec((tm, D_MODEL), lambda i: (i, 0))
    return pl.pallas_call(
        _mlp_kernel,
        grid=(n // tm,),
        in_specs=[row_spec] + _mlp_weight_specs(),
        out_specs=row_spec,
        out_shape=jax.ShapeDtypeStruct((n, D_MODEL), F32),
        compiler_params=pltpu.CompilerParams(
            dimension_semantics=("arbitrary",), vmem_limit_bytes=VMEM_LIMIT),
        name="mlp_ln",
    )(x, w_up, w_dn, g, b)


_GELU_C0 = (2.0 / jnp.pi) ** 0.5
_GELU_C1 = _GELU_C0 * 0.044715


def _gelu_tanh(x):
    t = jnp.tanh(x * (_GELU_C0 + _GELU_C1 * (x * x)))
    hx = 0.5 * x
    return hx + hx * t


def _lru_in_proj(x, wxy_ref, bxy_ref):
    uy = _dot(x.astype(BF16), wxy_ref[...]) + bxy_ref[...]
    cw2 = 2 * LRU_BLOCK
    n_col = D_RNN // cw2
    u = jnp.concatenate([uy[:, 2 * j * cw2:(2 * j + 1) * cw2] for j in range(n_col)], axis=1)
    y_pre = jnp.concatenate([uy[:, (2 * j + 1) * cw2:(2 * j + 2) * cw2] for j in range(n_col)], axis=1)
    return u, _gelu_tanh(y_pre)


def _neg_c_softplus_neg(lam):
    z = -lam
    sp = jnp.maximum(z, 0.0) + jnp.log1p(jnp.exp(-jnp.abs(z)))
    return -LRU_C * sp


def _lru_gate_chunk(xc_n, n, wg_ref, bg_ref, nla):
    return _lru_gate_math(_dot(xc_n.astype(BF16), wg_ref[n]), xc_n, n, bg_ref, nla)


def _lru_gate_math(gt, xc_n, n, bg_ref, nla):
    lo = n * LRU_BLOCK
    gt = gt + bg_ref[n]
    r = _sigmoid(gt[:, :LRU_BLOCK])
    gi = _sigmoid(gt[:, LRU_BLOCK:])
    log_a = r * nla[:, lo:lo + LRU_BLOCK]
    a = jnp.exp(log_a)
    v = -jnp.tanh(log_a) * (a * a + 1.0)
    mult = jnp.where(v > 0.0, v * lax.rsqrt(v), 0.0)
    return a, mult * (gi * xc_n)


def _lru_out(h, y, x, wo_ref, bo_ref, g_ref, b_ref):
    out = _dot((h * y).astype(BF16), wo_ref[...]) + bo_ref[...]
    return _layer_norm(ALPHA * x + out, g_ref[...], b_ref[...])


def _interleave_matrix(tt):
    r = jnp.arange(tt)
    src_time = (r % SUBLANES) * (tt // SUBLANES) + r // SUBLANES
    return (src_time[:, None] == jnp.arange(tt)[None, :]).astype(BF16)


def _groups(z):
    return [z[r:r + LRU_IL, :] for r in range(0, z.shape[0], LRU_IL)]


def _shift_one_step(z, carry_row):
    row = lax.broadcasted_iota(jnp.int32, (SUBLANES, z.shape[1]), 0)
    out = []
    for zg in _groups(z):
        last = zg[LRU_IL - SUBLANES:, :]
        out.append(jnp.where(row == 0, carry_row, pltpu.roll(last, 1, axis=0)))
        out.append(zg[:LRU_IL - SUBLANES, :])
        carry_row = zg[LRU_IL - 1:, :]
    return jnp.concatenate(out, axis=0)


def _scan_interleaved(a, b, h_in):
    parts = []
    for ag, bg in zip(_groups(a), _groups(b)):
        hg, h_in = _scan_group(ag, bg, h_in)
        parts.append(hg)
    return jnp.concatenate(parts, axis=0), h_in


def _scan_group(a, b, h_in):
    n_steps = a.shape[0] // SUBLANES
    h = b[0:SUBLANES, :]
    p = a[0:SUBLANES, :]
    hs, ps = [h], [p]
    for i in range(1, n_steps):
        ai = a[i * SUBLANES:(i + 1) * SUBLANES, :]
        h = ai * h + b[i * SUBLANES:(i + 1) * SUBLANES, :]
        p = ai * p
        hs.append(h)
        ps.append(p)
    row = lax.broadcasted_iota(jnp.int32, h.shape, 0)
    cin = h_in
    cvec = jnp.zeros(h.shape, F32)
    for c in range(SUBLANES):
        cvec = jnp.where(row == c, cin, cvec)
        cin = h[c:c + 1, :] + p[c:c + 1, :] * cin
    full = jnp.concatenate([hs[i] + ps[i] * cvec for i in range(n_steps)], axis=0)
    return full, cin


def _layer0_prompt_kernel(x_ref, pm_ref, pmt_ref, wxy_ref, bxy_ref, cw_ref, cb_ref, wg_ref, bg_ref, lam_ref,
                          wo_ref, bo_ref, g1_ref, b1_ref, wup_ref, wdn_ref, g2_ref, b2_ref,
                          y_ref, hout_ref, cout_ref,
                          hy_scr, hc_scr, cc_scr, ul_scr, x1_scr, z1_scr, acc_scr, *, tiles_per_seq):
    tt = LRU_TT
    s = pl.program_id(0)
    n_tiles = pl.num_programs(0) - 2
    t = s % tiles_per_seq
    slot = s % 2

    @pl.when(s == 0)
    def _():
        x1_scr[1] = jnp.zeros((tt, D_MODEL), F32)
        z1_scr[...] = jnp.zeros((tt, D_MODEL), F32)
        acc_scr[...] = jnp.zeros((tt, D_MODEL), F32)

    @pl.when(t == 0)
    def _():
        hc_scr[...] = jnp.zeros(hc_scr.shape, F32)
        cc_scr[...] = jnp.zeros(cc_scr.shape, F32)

    x = x_ref[...]
    xb = x.astype(BF16)
    acc = jnp.zeros((tt, D_MODEL), F32)
    nla = _neg_c_softplus_neg(lam_ref[...])
    n_mlp = LRU_BLOCKS
    fc = D_FF // n_mlp
    cw2 = 2 * LRU_BLOCK
    n_col = D_RNN // cw2

    def mlp_up(n):
        h = _dot(x1b, wup_ref[:, n * fc:(n + 1) * fc])
        return jnp.square(jnp.maximum(h, 0.0)).astype(BF16)

    def mlp_down(acc, h, n):
        return acc + _dot(h, wdn_ref[n * fc:(n + 1) * fc, :])

    def in_proj(j):
        return _dot(xpb, wxy_ref[:, 2 * j * cw2:2 * (j + 1) * cw2]) + bxy_ref[:, 2 * j * cw2:2 * (j + 1) * cw2]

    xp = jnp.concatenate([_dot(pm_ref[...], g) for g in _groups(xb)], axis=0)
    xpb = xp.astype(BF16)
    uy = in_proj(0)
    x1 = _layer_norm(z1_scr[...], g1_ref[...], b1_ref[...])
    x1_scr[slot] = x1
    x1b = x1.astype(BF16)
    hs = {}

    def mlp_ops(acc, ops):
        for kind, n in ops:
            if kind == "u":
                hs[n] = mlp_up(n)
            else:
                acc = mlp_down(acc, hs.pop(n), n)
        return acc

    phase_ops = [[("u", 1), ("d", 0), ("u", 2)], [("d", 1), ("u", 3), ("d", 2)],
                 [("u", 4), ("d", 3), ("u", 5)], [("d", 4), ("u", 6), ("d", 5)]]
    acc = mlp_ops(acc, [("u", 0)])
    for j in range(n_col):
        c0 = j * cw2
        acc = mlp_ops(acc, phase_ops[j])
        if j == 0:
            y_ref[...] = _layer_norm(ALPHA * x1_scr[1 - slot] + acc_scr[...], g2_ref[...], b2_ref[...])
        u = uy[:, :cw2]
        y = _gelu_tanh(uy[:, cw2:])
        z = cw_ref[0:1, c0:c0 + cw2] * u
        for k in range(1, CONV_W):
            carry = cc_scr[k - 1:k, c0:c0 + cw2]
            cc_scr[k - 1:k, c0:c0 + cw2] = z[tt - 1:tt, :]
            z = cw_ref[k:k + 1, c0:c0 + cw2] * u + _shift_one_step(z, carry)
        xc = z + cb_ref[:, c0:c0 + cw2]
        ul_scr[:, c0:c0 + cw2] = u[tt - (CONV_W - 1) * SUBLANES:, :]
        xcs = [xc[:, q * LRU_BLOCK:(q + 1) * LRU_BLOCK] for q in range(2)]
        gt2 = _dot(xc.astype(BF16), wg_ref[j])
        gts = [gt2[:, q * cw2:(q + 1) * cw2] for q in range(2)]
        if j + 1 < n_col:
            uy = in_proj(j + 1)
        for q in range(2):
            n = 2 * j + q
            lo = n * LRU_BLOCK
            a, b = _lru_gate_math(gts[q], xcs[q], n, bg_ref, nla)
            h, h_end = _scan_interleaved(a, b, hc_scr[0:1, lo:lo + LRU_BLOCK])
            hc_scr[0:1, lo:lo + LRU_BLOCK] = h_end
            hy_scr[:, lo:lo + LRU_BLOCK] = (h * y[:, q * LRU_BLOCK:(q + 1) * LRU_BLOCK]).astype(BF16)

    acc = mlp_ops(acc, [("u", 7), ("d", 6)])
    hy = jnp.concatenate([_dot(pmt_ref[...], g) for g in _groups(hy_scr[...])], axis=0)
    acc = mlp_ops(acc, [("d", 7)])
    out = _dot(hy.astype(BF16), wo_ref[...]) + bo_ref[...]
    assert not hs and n_mlp == 8 and n_col == len(phase_ops)
    z1_scr[...] = ALPHA * x + out
    acc_scr[...] = acc

    @pl.when(jnp.logical_and(t == tiles_per_seq - 1, s < n_tiles))
    def _():
        hout_ref[...] = hc_scr[0:1, :]
        for k in range(CONV_W - 1):
            r = (k + 1) * SUBLANES - 1
            cout_ref[k:k + 1, :] = ul_scr[r:r + 1, :]


def _lru_sample_kernel(x_ref, h0_ref, cs_ref, wxy_ref, bxy_ref, cw_ref, cb_ref, wg_ref, bg_ref,
                       lam_ref, wo_ref, bo_ref, g_ref, b_ref,
                       y_ref, hout_ref, cout_ref, *, nb, nt):
    x = x_ref[...]
    u, y = _lru_in_proj(x, wxy_ref, bxy_ref)
    upad = jnp.concatenate([cs_ref[...], u], axis=0)
    xc = cb_ref[...]
    for k in range(CONV_W):
        xc = xc + cw_ref[k:k + 1, :] * upad[k * nb:(k + nt) * nb, :]
    nla = _neg_c_softplus_neg(lam_ref[...])
    a_parts, b_parts = [], []
    for n in range(LRU_BLOCKS):
        lo = n * LRU_BLOCK
        a, b = _lru_gate_chunk(xc[:, lo:lo + LRU_BLOCK], n, wg_ref, bg_ref, nla)
        a_parts.append(a)
        b_parts.append(b)
    a = jnp.concatenate(a_parts, axis=1)
    b = jnp.concatenate(b_parts, axis=1)
    h = h0_ref[...]
    hs = []
    for t in range(nt):
        h = a[t * nb:(t + 1) * nb, :] * h + b[t * nb:(t + 1) * nb, :]
        hs.append(h)
    y_ref[...] = _lru_out(jnp.concatenate(hs, axis=0), y, x, wo_ref, bo_ref, g_ref, b_ref)
    hout_ref[...] = h
    cout_ref[...] = upad[nt * nb:, :]


def _lru_weight_specs(paired_gates=False):
    pair = 2 if paired_gates else 1
    return [_const_spec((D_MODEL, 2 * D_RNN)), _const_spec((1, 2 * D_RNN)),
            _const_spec((CONV_W, D_RNN)), _const_spec((1, D_RNN)),
            _const_spec((LRU_BLOCKS // pair, pair * LRU_BLOCK, pair * 2 * LRU_BLOCK)),
            _const_spec((LRU_BLOCKS, 1, 2 * LRU_BLOCK)),
            _const_spec((1, D_RNN)), _const_spec((D_RNN, D_MODEL)), _const_spec((1, D_MODEL)),
            _const_spec((1, D_MODEL)), _const_spec((1, D_MODEL))]


def _layer0_prompt(x, lru_weights, mlp_weights):
    bsz, seq, _ = x.shape
    tt = LRU_TT
    assert seq % tt == 0
    tps = seq // tt
    n_tiles = bsz * tps

    def lru_tile(s):
        s = jnp.minimum(s, n_tiles - 1)
        return (s // tps, s % tps, 0)

    def mlp_tile(s):
        s = jnp.maximum(s - 2, 0)
        return (s // tps, s % tps, 0)

    def state_block(s):
        return (jnp.minimum(s, n_tiles - 1) // tps, 0, 0)

    assert tt % LRU_IL == 0
    pm = _interleave_matrix(LRU_IL)
    return pl.pallas_call(
        functools.partial(_layer0_prompt_kernel, tiles_per_seq=tps),
        grid=(n_tiles + 2,),
        in_specs=[pl.BlockSpec((None, tt, D_MODEL), lru_tile), _const_spec((LRU_IL, LRU_IL)),
                  _const_spec((LRU_IL, LRU_IL))]
        + _lru_weight_specs(paired_gates=True) + _mlp_weight_specs(),
        out_specs=[pl.BlockSpec((None, tt, D_MODEL), mlp_tile),
                   pl.BlockSpec((None, 1, D_RNN), state_block),
                   pl.BlockSpec((None, CONV_W - 1, D_RNN), state_block)],
        out_shape=[jax.ShapeDtypeStruct((bsz, seq, D_MODEL), F32),
                   jax.ShapeDtypeStruct((bsz, 1, D_RNN), F32),
                   jax.ShapeDtypeStruct((bsz, CONV_W - 1, D_RNN), F32)],
        scratch_shapes=[pltpu.VMEM((tt, D_RNN), BF16),
                        pltpu.VMEM((SUBLANES, D_RNN), F32),
                        pltpu.VMEM((SUBLANES, D_RNN), F32),
                        pltpu.VMEM(((CONV_W - 1) * SUBLANES, D_RNN), F32),
                        pltpu.VMEM((2, tt, D_MODEL), F32),
                        pltpu.VMEM((tt, D_MODEL), F32),
                        pltpu.VMEM((tt, D_MODEL), F32)],
        compiler_params=pltpu.CompilerParams(
            dimension_semantics=("arbitrary",), vmem_limit_bytes=VMEM_LIMIT),
        name="layer0_prompt",
    )(x, pm, pm.T, *lru_weights, *mlp_weights)


def _lru_sample(x_tm, h0, cs_tm, weights, nb, nt):
    rows = nb * nt
    return pl.pallas_call(
        functools.partial(_lru_sample_kernel, nb=nb, nt=nt),
        grid=(1,),
        in_specs=[_const_spec((rows, D_MODEL)), _const_spec((nb, D_RNN)),
                  _const_spec(((CONV_W - 1) * nb, D_RNN))] + _lru_weight_specs(),
        out_specs=[_const_spec((rows, D_MODEL)), _const_spec((nb, D_RNN)),
                   _const_spec(((CONV_W - 1) * nb, D_RNN))],
        out_shape=[jax.ShapeDtypeStruct((rows, D_MODEL), F32),
                   jax.ShapeDtypeStruct((nb, D_RNN), F32),
                   jax.ShapeDtypeStruct(((CONV_W - 1) * nb, D_RNN), F32)],
        compiler_params=pltpu.CompilerParams(
            dimension_semantics=("arbitrary",), vmem_limit_bytes=VMEM_LIMIT),
        name="lru_sample",
    )(x_tm, h0, cs_tm, *weights)


def _lane_is_kv0(shape):
    return lax.broadcasted_iota(jnp.int32, shape, len(shape) - 1) < HEAD_DIM


def _sink_softmax(s, mask, sink):
    s = jnp.where(mask, s, -jnp.inf)
    m = jnp.maximum(jnp.max(s, axis=-1, keepdims=True), sink)
    p = jnp.exp(s - m)
    denom = jnp.sum(p, axis=-1, keepdims=True) + jnp.exp(sink - m)
    return p.astype(BF16), 1.0 / denom


def _layer1_prompt_kernel(sink_ref, x_ref, wqkv_ref, bqkv_ref, wo_ref, bo_ref, g1_ref, b1_ref,
                          wup_ref, wdn_ref, g2_ref, b2_ref,
                          y_ref, kout_ref, vout_ref,
                          q_scr, k0_scr, k1_scr, v_scr, o_scr, kl_scr, vl_scr, x1_scr, z1_scr, acc_scr,
                          *, tiles_per_seq):
    tt = SWA_TT
    blk = WINDOW
    s_id = pl.program_id(0)
    n_tiles = pl.num_programs(0) - 2
    t = s_id % tiles_per_seq
    slot = s_id % 2

    @pl.when(s_id == 0)
    def _():
        x1_scr[1] = jnp.zeros((tt, D_MODEL), F32)
        z1_scr[...] = jnp.zeros((tt, D_MODEL), F32)
        acc_scr[...] = jnp.zeros((tt, D_MODEL), F32)

    @pl.when(t == 0)
    def _():
        zeros = jnp.zeros((blk, KV_DIM), BF16)
        k0_scr[0:blk, :] = zeros
        k1_scr[0:blk, :] = zeros
        v_scr[0:blk, :] = zeros

    x = x_ref[...]
    xb = x.astype(BF16)
    acc = jnp.zeros((tt, D_MODEL), F32)
    n_mlp = L1_MLP_CHUNKS
    fc = D_FF // n_mlp
    hs = {}
    nxt_up = [0]
    nxt_dn = [0]

    def mlp_up():
        n = nxt_up[0]
        nxt_up[0] += 1
        h = _dot(x1b, wup_ref[:, n * fc:(n + 1) * fc])
        hs[n] = jnp.square(jnp.maximum(h, 0.0)).astype(BF16)

    def mlp_down(acc):
        n = nxt_dn[0]
        nxt_dn[0] += 1
        return acc + _dot(hs.pop(n), wdn_ref[n * fc:(n + 1) * fc, :])

    def mlp_fill(acc):
        if nxt_up[0] < n_mlp:
            mlp_up()
        if nxt_dn[0] < nxt_up[0] - 1 or (nxt_up[0] == n_mlp and nxt_dn[0] < n_mlp):
            acc = mlp_down(acc)
        return acc

    qkv =_dot(xb, wqkv_ref[...]) + bqkv_ref[...]
    x1 = _layer_norm(z1_scr[...], g1_ref[...], b1_ref[...])
    x1_scr[slot] = x1
    x1b = x1.astype(BF16)
    mlp_up()
    acc = mlp_fill(acc)
    y_ref[...] = _layer_norm(ALPHA * x1_scr[1 - slot] + acc_scr[...], g2_ref[...], b2_ref[...])
    k = qkv[:, Q_DIM:Q_DIM + KV_DIM]
    v = qkv[:, Q_DIM + KV_DIM:]
    q_scr[...] = (qkv[:, :Q_DIM] * (HEAD_DIM ** -0.5)).astype(BF16)
    kv0 = _lane_is_kv0((tt, KV_DIM))
    k0_scr[blk:, :] = jnp.where(kv0, k, 0.0).astype(BF16)
    k1_scr[blk:, :] = jnp.where(kv0, 0.0, k).astype(BF16)
    v_scr[blk:, :] = v.astype(BF16)
    kl_scr[...] = k[tt - blk:, :]
    vl_scr[...] = v[tt - blk:, :]

    qi = lax.broadcasted_iota(jnp.int32, (blk, 2 * blk), 0)
    kj = lax.broadcasted_iota(jnp.int32, (blk, 2 * blk), 1)
    band = (kj > qi) & (kj <= qi + blk)
    kv0_o = _lane_is_kv0((blk, KV_DIM))

    n_blk = tt // blk
    scores = []
    for i in range(n_blk):
        r0 = i * blk
        qs = jnp.concatenate(
            [q_scr[r0:r0 + blk, g * KV_DIM:(g + 1) * KV_DIM] for g in range(GROUP)], axis=0)
        kband = jnp.concatenate([k0_scr[r0:r0 + 2 * blk, :], k1_scr[r0:r0 + 2 * blk, :]], axis=0)
        scores.append(_dot_nt(qs, kband))
        acc = mlp_fill(acc)
    for i in range(n_blk):
        first = jnp.logical_and(t == 0, i == 0)
        mask = band & (kj >= jnp.where(first, blk, 0))
        r0 = i * blk
        vb = v_scr[r0:r0 + 2 * blk, :]
        ps, rs = [], []
        for kh in range(N_KV):
            s_kh = scores[i][:, kh * 2 * blk:(kh + 1) * 2 * blk]
            for g in range(GROUP):
                p, r = _sink_softmax(s_kh[g * blk:(g + 1) * blk, :], mask, sink_ref[kh * GROUP + g])
                ps.append(p)
                rs.append(r)
        o = _dot(jnp.concatenate(ps, axis=0), vb)
        acc = mlp_fill(acc)
        for g in range(GROUP):
            o0 = o[g * blk:(g + 1) * blk, :] * rs[g]
            o1 = o[(GROUP + g) * blk:(GROUP + g + 1) * blk, :] * rs[GROUP + g]
            o_scr[r0:r0 + blk, g * KV_DIM:(g + 1) * KV_DIM] = jnp.where(kv0_o, o0, o1).astype(BF16)

    out =_dot(o_scr[...], wo_ref[...]) + bo_ref[...]
    while nxt_dn[0] < n_mlp:
        acc = mlp_fill(acc)
    assert not hs and nxt_up[0] == n_mlp
    z1_scr[...] = ALPHA * x + out
    acc_scr[...] = acc

    k0_scr[0:blk, :] = k0_scr[tt:tt + blk, :]
    k1_scr[0:blk, :] = k1_scr[tt:tt + blk, :]
    v_scr[0:blk, :] = v_scr[tt:tt + blk, :]

    @pl.when(jnp.logical_and(t == tiles_per_seq - 1, s_id < n_tiles))
    def _():
        kout_ref[...] = kl_scr[...]
        vout_ref[...] = vl_scr[...]


def _swa_weight_specs():
    return [_const_spec((D_MODEL, QKV_DIM)), _const_spec((1, QKV_DIM)),
            _const_spec((Q_DIM, D_MODEL)), _const_spec((1, D_MODEL)),
            _const_spec((1, D_MODEL)), _const_spec((1, D_MODEL))]


def _layer1_prompt(x, sinks, swa_weights, mlp_weights):
    bsz, seq, _ = x.shape
    tt = SWA_TT
    assert seq % tt == 0 and seq >= WINDOW and tt % WINDOW == 0
    tps = seq // tt
    n_tiles = bsz * tps

    def swa_tile(s):
        s = jnp.minimum(s, n_tiles - 1)
        return (s // tps, s % tps, 0)

    def mlp_tile(s):
        s = jnp.maximum(s - 2, 0)
        return (s // tps, s % tps, 0)

    kv_spec = pl.BlockSpec((None, WINDOW, KV_DIM), lambda s: (jnp.minimum(s, n_tiles - 1) // tps, 0, 0))
    return pl.pallas_call(
        functools.partial(_layer1_prompt_kernel, tiles_per_seq=tps),
        grid=(n_tiles + 2,),
        in_specs=[pl.BlockSpec(memory_space=pltpu.SMEM), pl.BlockSpec((None, tt, D_MODEL), swa_tile)]
        + _swa_weight_specs() + _mlp_weight_specs(),
        out_specs=[pl.BlockSpec((None, tt, D_MODEL), mlp_tile), kv_spec, kv_spec],
        out_shape=[jax.ShapeDtypeStruct((bsz, seq, D_MODEL), F32),
                   jax.ShapeDtypeStruct((bsz, WINDOW, KV_DIM), F32),
                   jax.ShapeDtypeStruct((bsz, WINDOW, KV_DIM), F32)],
        scratch_shapes=[pltpu.VMEM((tt, Q_DIM), BF16),
                        pltpu.VMEM((tt + WINDOW, KV_DIM), BF16),
                        pltpu.VMEM((tt + WINDOW, KV_DIM), BF16),
                        pltpu.VMEM((tt + WINDOW, KV_DIM), BF16),
                        pltpu.VMEM((tt, Q_DIM), BF16),
                        pltpu.VMEM((WINDOW, KV_DIM), F32),
                        pltpu.VMEM((WINDOW, KV_DIM), F32),
                        pltpu.VMEM((2, tt, D_MODEL), F32),
                        pltpu.VMEM((tt, D_MODEL), F32),
                        pltpu.VMEM((tt, D_MODEL), F32)],
        compiler_params=pltpu.CompilerParams(
            dimension_semantics=("arbitrary",), vmem_limit_bytes=VMEM_LIMIT),
        name="layer1_prompt",
    )(sinks, x, *swa_weights, *mlp_weights)


def _swa_sample_kernel(sink_ref, x_ref, ck_ref, cv_ref, wqkv_ref, bqkv_ref, wo_ref, bo_ref, g_ref, b_ref,
                       y_ref, ks_ref, vs_ref,
                       q_scr, knew_scr, vnew_scr, o_scr, *, nt):
    pt = SMP_PAD_T
    wbuf = ck_ref.shape[1]
    nkeys = 2 * WINDOW
    step = pl.program_id(0)

    @pl.when(step == 0)
    def _():
        qkv = _dot(x_ref[...].astype(BF16), wqkv_ref[...]) + bqkv_ref[...]
        q_scr[...] = qkv[:, :Q_DIM] * (HEAD_DIM ** -0.5)
        knew_scr[...] = qkv[:, Q_DIM:Q_DIM + KV_DIM]
        vnew_scr[...] = qkv[:, Q_DIM + KV_DIM:]

    rows = GROUP * pt
    ri = lax.broadcasted_iota(jnp.int32, (rows, nkeys), 0)
    kj = lax.broadcasted_iota(jnp.int32, (rows, nkeys), 1)
    tq = ri % pt
    in_buf = kj < wbuf
    mask = (in_buf & (kj - wbuf + WINDOW > tq)) | (
        jnp.logical_not(in_buf) & (kj - wbuf <= tq) & (kj - wbuf < nt))
    gi = lax.broadcasted_iota(jnp.int32, (rows, 1), 0) // pt
    sink_cols = []
    for kh in range(N_KV):
        col = jnp.zeros((rows, 1), F32)
        for g in range(GROUP):
            col = jnp.where(gi == g, sink_ref[kh * GROUP + g], col)
        sink_cols.append(col)
    kv0_k = _lane_is_kv0((nkeys, KV_DIM))
    kv0_o = _lane_is_kv0((rows, KV_DIM))

    pad_rows = jnp.zeros((nkeys - wbuf - pt, KV_DIM), F32)
    tail_is_old = lax.broadcasted_iota(jnp.int32, (pt, KV_DIM), 0) < pt - nt

    def per_group(gi, carry):
        bs = [gi * SMP_GROUP + i for i in range(SMP_GROUP)]
        staged = []
        for b in bs:
            r0 = pl.multiple_of((step * SMP_BB + b) * pt, pt)
            k_new = knew_scr[pl.ds(r0, pt), :]
            v_new = vnew_scr[pl.ds(r0, pt), :]
            kb = jnp.concatenate([ck_ref[b], k_new, pad_rows], axis=0)
            vb = jnp.concatenate([cv_ref[b], v_new, pad_rows], axis=0).astype(BF16)
            qs = jnp.concatenate(
                [q_scr[pl.ds(r0, pt), g * KV_DIM:(g + 1) * KV_DIM] for g in range(GROUP)], axis=0)
            qs = qs.astype(BF16)
            scores = [_dot_nt(qs, jnp.where(kv0_k, kb, 0.0).astype(BF16)),
                      _dot_nt(qs, jnp.where(kv0_k, 0.0, kb).astype(BF16))]
            staged.append((b, r0, k_new, v_new, vb, scores))
        soft = [[_sink_softmax(sc[kh], mask, sink_cols[kh]) for kh in range(N_KV)]
                for (_, _, _, _, _, sc) in staged]
        for (b, r0, k_new, v_new, vb, _), pr in zip(staged, soft):
            o0 = _dot(pr[0][0], vb) * pr[0][1]
            o1 = _dot(pr[1][0], vb) * pr[1][1]
            o = jnp.where(kv0_o, o0, o1)
            for g in range(GROUP):
                o_scr[pl.ds(r0, pt), g * KV_DIM:(g + 1) * KV_DIM] = o[g * pt:(g + 1) * pt, :]
            for new, c_ref, out_ref in ((k_new, ck_ref, ks_ref), (v_new, cv_ref, vs_ref)):
                out_ref[b, 0:wbuf - pt, :] = c_ref[b, nt:wbuf - pt + nt, :]
                old_tail = pltpu.roll(c_ref[b, wbuf - pt:wbuf, :], pt - nt, axis=0)
                out_ref[b, wbuf - pt:wbuf, :] = jnp.where(
                    tail_is_old, old_tail, pltpu.roll(new, pt - nt, axis=0))
        return carry

    lax.fori_loop(0, SMP_BB // SMP_GROUP, per_group, 0)

    @pl.when(step == pl.num_programs(0) - 1)
    def _():
        out = _dot(o_scr[...].astype(BF16), wo_ref[...]) + bo_ref[...]
        y_ref[...] = _layer_norm(ALPHA * x_ref[...] + out, g_ref[...], b_ref[...])


def _swa_sample(x_pad, cache_k, cache_v, sinks, weights, nt):
    rows = x_pad.shape[0]
    nb, wbuf, _ = cache_k.shape
    assert nb % SMP_BB == 0 and rows == nb * SMP_PAD_T and nt <= SMP_PAD_T
    assert wbuf + SMP_PAD_T <= 2 * WINDOW and wbuf % SMP_PAD_T == 0
    cache_spec = pl.BlockSpec((SMP_BB, wbuf, KV_DIM), lambda i: (i, 0, 0))
    return pl.pallas_call(
        functools.partial(_swa_sample_kernel, nt=nt),
        grid=(nb // SMP_BB,),
        in_specs=[pl.BlockSpec(memory_space=pltpu.SMEM), _const_spec((rows, D_MODEL)),
                  cache_spec, cache_spec] + _swa_weight_specs(),
        out_specs=[_const_spec((rows, D_MODEL)), cache_spec, cache_spec],
        out_shape=[jax.ShapeDtypeStruct((rows, D_MODEL), F32),
                   jax.ShapeDtypeStruct((nb, wbuf, KV_DIM), F32),
                   jax.ShapeDtypeStruct((nb, wbuf, KV_DIM), F32)],
        scratch_shapes=[pltpu.VMEM((rows, Q_DIM), F32),
                        pltpu.VMEM((rows, KV_DIM), F32),
                        pltpu.VMEM((rows, KV_DIM), F32),
                        pltpu.VMEM((rows, Q_DIM), F32)],
        compiler_params=pltpu.CompilerParams(
            dimension_semantics=("arbitrary",), vmem_limit_bytes=VMEM_LIMIT),
        name="swa_sample",
    )(sinks, x_pad, cache_k, cache_v, *weights)


def _row(v):
    return v.reshape(1, -1).astype(F32)


def _pack_lru_weights(j, lru_w_x, lru_b_x, lru_w_y, lru_b_y, lru_conv_w, lru_conv_b,
                      lru_w_ga, lru_b_ga, lru_w_gi, lru_b_gi, lru_lam, lru_w_out, lru_b_out, g, b):
    def chunk_pack(ax, ay):
        lead = ax.shape[:-1]
        cw2 = 2 * LRU_BLOCK
        return jnp.concatenate([ax.reshape(*lead, -1, cw2), ay.reshape(*lead, -1, cw2)],
                               axis=-1).reshape(*lead, 2 * D_RNN)

    wxy = chunk_pack(lru_w_x[j].astype(BF16), lru_w_y[j].astype(BF16))
    bxy = chunk_pack(lru_b_x[j], lru_b_y[j]).reshape(1, -1)
    wg = jnp.concatenate([lru_w_ga[j], lru_w_gi[j]], axis=2).astype(BF16)
    bg = jnp.concatenate([lru_b_ga[j], lru_b_gi[j]], axis=1)[:, None, :]
    return (wxy, bxy, lru_conv_w[j], _row(lru_conv_b[j]), wg, bg, _row(lru_lam[j]),
            lru_w_out[j].astype(BF16), _row(lru_b_out[j]), _row(g), _row(b))


def _pack_swa_weights(j, swa_w_qkv, swa_b_qkv, swa_w_o, swa_b_o, g, b):
    w, bias = swa_w_qkv[j].astype(BF16), swa_b_qkv[j]
    wq = w[:, :Q_DIM].reshape(D_MODEL, N_KV, GROUP, HEAD_DIM).transpose(0, 2, 1, 3).reshape(D_MODEL, Q_DIM)
    bq = bias[:Q_DIM].reshape(N_KV, GROUP, HEAD_DIM).transpose(1, 0, 2).reshape(Q_DIM)
    wqkv = jnp.concatenate([wq, w[:, Q_DIM:]], axis=1)
    bqkv = jnp.concatenate([bq, bias[Q_DIM:]]).reshape(1, -1)
    wo = swa_w_o[j].astype(BF16).reshape(N_KV, GROUP, HEAD_DIM, D_MODEL).transpose(1, 0, 2, 3)
    return (wqkv, bqkv, wo.reshape(Q_DIM, D_MODEL), _row(swa_b_o[j]), _row(g), _row(b))


def kernel(x_prompt, x_sample, state_lru_h, state_lru_conv, cache_swa_k, cache_swa_v,
           lru_w_x, lru_b_x, lru_w_y, lru_b_y, lru_conv_w, lru_conv_b,
           lru_w_ga, lru_b_ga, lru_w_gi, lru_b_gi, lru_lam, lru_w_out, lru_b_out,
           swa_w_qkv, swa_b_qkv, swa_sinks, swa_w_o, swa_b_o,
           mlp_w_up, mlp_w_down, ln1_g, ln1_b, ln2_g, ln2_b):
    bsz, seq, _ = x_prompt.shape
    nb, nt, _ = x_sample.shape
    wbuf = cache_swa_k.shape[2]

    mlp_w = [(mlp_w_up[i].astype(BF16), mlp_w_down[i].astype(BF16), _row(ln2_g[i]), _row(ln2_b[i]))
             for i in range(DEPTH)]

    def mlp(x2d, i):
        return _mlp_block(x2d, *mlp_w[i])

    lw = _pack_lru_weights(0, lru_w_x, lru_b_x, lru_w_y, lru_b_y, lru_conv_w, lru_conv_b,
                           lru_w_ga, lru_b_ga, lru_w_gi, lru_b_gi, lru_lam, lru_w_out, lru_b_out,
                           ln1_g[0], ln1_b[0])
    wg = lw[4]
    zero = jnp.zeros_like(wg[0::2])
    wg_pairs = jnp.concatenate([jnp.concatenate([wg[0::2], zero], axis=2),
                                jnp.concatenate([zero, wg[1::2]], axis=2)], axis=1)
    lw_p = lw[:4] + (wg_pairs,) + lw[5:]
    xp, h_p, conv_p = _layer0_prompt(x_prompt, lw_p, mlp_w[0])

    xs_tm = x_sample.transpose(1, 0, 2).reshape(nt * nb, D_MODEL)
    cs_tm = state_lru_conv[0].transpose(1, 0, 2).reshape((CONV_W - 1) * nb, D_RNN)
    xs_tm, h_s, conv_s_tm = _lru_sample(xs_tm, state_lru_h[0], cs_tm, lw, nb, nt)
    xs_tm = mlp(xs_tm, 0)
    conv_s = conv_s_tm.reshape(CONV_W - 1, nb, D_RNN).transpose(1, 0, 2)

    sw = _pack_swa_weights(0, swa_w_qkv, swa_b_qkv, swa_w_o, swa_b_o, ln1_g[1], ln1_b[1])
    sinks = swa_sinks[0].astype(F32)
    xp, k_p, v_p = _layer1_prompt(xp, sinks, sw, mlp_w[1])

    xs_bm = xs_tm.reshape(nt, nb, D_MODEL).transpose(1, 0, 2)
    xs_pad = jnp.pad(xs_bm, ((0, 0), (0, SMP_PAD_T - nt), (0, 0))).reshape(nb * SMP_PAD_T, D_MODEL)
    ck = cache_swa_k[0].reshape(nb, wbuf, KV_DIM)
    cv = cache_swa_v[0].reshape(nb, wbuf, KV_DIM)
    ys_pad, k_s, v_s = _swa_sample(xs_pad, ck, cv, sinks, sw, nt)
    xs = ys_pad.reshape(nb, SMP_PAD_T, D_MODEL)[:, :nt].reshape(nb * nt, D_MODEL)
    xs = mlp(xs, 1).reshape(nb, nt, D_MODEL)

    kv_shape = (1, -1, min(WINDOW, seq), N_KV, HEAD_DIM)
    return (xp, xs,
            h_p.reshape(1, bsz, D_RNN), conv_p[None],
            k_p.reshape(kv_shape), v_p.reshape(kv_shape),
            h_s[None], conv_s[None],
            k_s.reshape(1, nb, wbuf, N_KV, HEAD_DIM), v_s.reshape(1, nb, wbuf, N_KV, HEAD_DIM))
```

```python
import functools

import jax
import jax.numpy as jnp
from jax import lax
from jax.experimental import pallas as pl
from jax.experimental.pallas import tpu as pltpu

F32 = jnp.float32
BF16 = jnp.bfloat16

D_MODEL = 1024
D_RNN = 1024
D_FF = 4096
LRU_BLOCKS = 8
LRU_BLOCK = 128
CONV_W = 4
LRU_C = 8.0
N_HEADS = 16
HEAD_DIM = 64
N_KV = 2
GROUP = 8
WINDOW = 128
KV_DIM = N_KV * HEAD_DIM
Q_DIM = N_HEADS * HEAD_DIM
QKV_DIM = Q_DIM + 2 * KV_DIM
DEPTH = 2
ALPHA = (2.0 * DEPTH) ** 0.25
LN_EPS = 1e-5

SUBLANES = 8
VMEM_LIMIT = 60 * 1024 * 1024

MLP_TM = 512
MLP_FC = 1024
LRU_TT = 512
LRU_IL = 256
SWA_TT = 512
SWA_PAIR = 2
L1_MLP_CHUNKS = 8
SMP_PAD_T = 8
SMP_BB = 16
SMP_GROUP = 4


def _const_spec(shape):
    nd = len(shape)
    return pl.BlockSpec(shape, lambda *_: (0,) * nd, pipeline_mode=pl.Buffered(1))


def _dot(a, b):
    return jnp.dot(a, b, preferred_element_type=F32)


def _dot_nt(a, b):
    return lax.dot_general(a, b, (((1,), (1,)), ((), ())), preferred_element_type=F32)


def _layer_norm(z, g, b):
    mu = jnp.mean(z, axis=-1, keepdims=True)
    d = z - mu
    var = jnp.mean(d * d, axis=-1, keepdims=True)
    return d * lax.rsqrt(var + LN_EPS) * g + b


def _sigmoid(x):
    return 0.5 * jnp.tanh(0.5 * x) + 0.5


def _mlp_ln(x, wup_ref, wdn_ref, g_ref, b_ref):
    xb = x.astype(BF16)
    acc = jnp.zeros(x.shape, F32)
    for j in range(D_FF // MLP_FC):
        h = _dot(xb, wup_ref[:, j * MLP_FC:(j + 1) * MLP_FC])
        h = jnp.square(jnp.maximum(h, 0.0)).astype(BF16)
        acc = acc + _dot(h, wdn_ref[j * MLP_FC:(j + 1) * MLP_FC, :])
    return _layer_norm(ALPHA * x + acc, g_ref[...], b_ref[...])


def _mlp_kernel(x_ref, wup_ref, wdn_ref, g_ref, b_ref, o_ref):
    o_ref[...] = _mlp_ln(x_ref[...], wup_ref, wdn_ref, g_ref, b_ref)


def _mlp_weight_specs():
    return [_const_spec((D_MODEL, D_FF)), _const_spec((D_FF, D_MODEL)),
            _const_spec((1, D_MODEL)), _const_spec((1, D_MODEL))]


def _mlp_block(x, w_up, w_dn, g, b):
    n = x.shape[0]
    tm = min(MLP_TM, n)
    assert n % tm == 0
    row_spec = pl.BlockSpec((tm, D_MODEL), lambda i: (i, 0))
    return pl.pallas_call(
        _mlp_kernel,
        grid=(n // tm,),
        in_specs=[row_spec] + _mlp_weight_specs(),
        out_specs=row_spec,
        out_shape=jax.ShapeDtypeStruct((n, D_MODEL), F32),
        compiler_params=pltpu.CompilerParams(
            dimension_semantics=("arbitrary",), vmem_limit_bytes=VMEM_LIMIT),
        name="mlp_ln",
    )(x, w_up, w_dn, g, b)


_GELU_C0 = (2.0 / jnp.pi) ** 0.5
_GELU_C1 = _GELU_C0 * 0.044715


def _gelu_tanh(x):
    t = jnp.tanh(x * (_GELU_C0 + _GELU_C1 * (x * x)))
    hx = 0.5 * x
    return hx + hx * t


def _lru_in_proj(x, wxy_ref, bxy_ref):
    uy = _dot(x.astype(BF16), wxy_ref[...]) + bxy_ref[...]
    cw2 = 2 * LRU_BLOCK
    n_col = D_RNN // cw2
    u = jnp.concatenate([uy[:, 2 * j * cw2:(2 * j + 1) * cw2] for j in range(n_col)], axis=1)
    y_pre = jnp.concatenate([uy[:, (2 * j + 1) * cw2:(2 * j + 2) * cw2] for j in range(n_col)], axis=1)
    return u, _gelu_tanh(y_pre)


def _neg_c_softplus_neg(lam):
    z = -lam
    sp = jnp.maximum(z, 0.0) + jnp.log1p(jnp.exp(-jnp.abs(z)))
    return -LRU_C * sp


def _lru_gate_chunk(xc_n, n, wg_ref, bg_ref, nla):
    return _lru_gate_math(_dot(xc_n.astype(BF16), wg_ref[n]), xc_n, n, bg_ref, nla)


def _lru_gate_math(gt, xc_n, n, bg_ref, nla):
    lo = n * LRU_BLOCK
    gt = gt + bg_ref[n]
    r = _sigmoid(gt[:, :LRU_BLOCK])
    gi = _sigmoid(gt[:, LRU_BLOCK:])
    log_a = r * nla[:, lo:lo + LRU_BLOCK]
    a = jnp.exp(log_a)
    v = -jnp.tanh(log_a) * (a * a + 1.0)
    mult = jnp.where(v > 0.0, v * lax.rsqrt(v), 0.0)
    return a, mult * (gi * xc_n)


def _lru_out(h, y, x, wo_ref, bo_ref, g_ref, b_ref):
    out = _dot((h * y).astype(BF16), wo_ref[...]) + bo_ref[...]
    return _layer_norm(ALPHA * x + out, g_ref[...], b_ref[...])


def _interleave_matrix(tt):
    r = jnp.arange(tt)
    src_time = (r % SUBLANES) * (tt // SUBLANES) + r // SUBLANES
    return (src_time[:, None] == jnp.arange(tt)[None, :]).astype(BF16)


def _groups(z):
    return [z[r:r + LRU_IL, :] for r in range(0, z.shape[0], LRU_IL)]


def _shift_one_step(z, carry_row):
    row = lax.broadcasted_iota(jnp.int32, (SUBLANES, z.shape[1]), 0)
    out = []
    for zg in _groups(z):
        last = zg[LRU_IL - SUBLANES:, :]
        out.append(jnp.where(row == 0, carry_row, pltpu.roll(last, 1, axis=0)))
        out.append(zg[:LRU_IL - SUBLANES, :])
        carry_row = zg[LRU_IL - 1:, :]
    return jnp.concatenate(out, axis=0)


def _scan_interleaved(a, b, h_in):
    parts = []
    for ag, bg in zip(_groups(a), _groups(b)):
        hg, h_in = _scan_group(ag, bg, h_in)
        parts.append(hg)
    return jnp.concatenate(parts, axis=0), h_in


def _scan_group(a, b, h_in):
    n_steps = a.shape[0] // SUBLANES
    h = b[0:SUBLANES, :]
    p = a[0:SUBLANES, :]
    hs, ps = [h], [p]
    for i in range(1, n_steps):
        ai = a[i * SUBLANES:(i + 1) * SUBLANES, :]
        h = ai * h + b[i * SUBLANES:(i + 1) * SUBLANES, :]
        p = ai * p
        hs.append(h)
        ps.append(p)
    row = lax.broadcasted_iota(jnp.int32, h.shape, 0)
    cin = h_in
    cvec = jnp.zeros(h.shape, F32)
    for c in range(SUBLANES):
        cvec = jnp.where(row == c, cin, cvec)
        cin = h[c:c + 1, :] + p[c:c + 1, :] * cin
    full = jnp.concatenate([hs[i] + ps[i] * cvec for i in range(n_steps)], axis=0)
    return full, cin


def _layer0_prompt_kernel(x_ref, pm_ref, pmt_ref, wxy_ref, bxy_ref, cw_ref, cb_ref, wg_ref, bg_ref, lam_ref,
                          wo_ref, bo_ref, g1_ref, b1_ref, wup_ref, wdn_ref, g2_ref, b2_ref,
                          y_ref, hout_ref, cout_ref,
                          hy_scr, hc_scr, cc_scr, ul_scr, x1_scr, z1_scr, acc_scr, *, tiles_per_seq):
    tt = LRU_TT
    s = pl.program_id(0)
    n_tiles = pl.num_programs(0) - 2
    t = s % tiles_per_seq
    slot = s % 2

    @pl.when(s == 0)
    def _():
        x1_scr[1] = jnp.zeros((tt, D_MODEL), F32)
        z1_scr[...] = jnp.zeros((tt, D_MODEL), F32)
        acc_scr[...] = jnp.zeros((tt, D_MODEL), F32)

    @pl.when(t == 0)
    def _():
        hc_scr[...] = jnp.zeros(hc_scr.shape, F32)
        cc_scr[...] = jnp.zeros(cc_scr.shape, F32)

    x = x_ref[...]
    xb = x.astype(BF16)
    acc = jnp.zeros((tt, D_MODEL), F32)
    nla = _neg_c_softplus_neg(lam_ref[...])
    n_mlp = LRU_BLOCKS
    fc = D_FF // n_mlp
    cw2 = 2 * LRU_BLOCK
    n_col = D_RNN // cw2

    def mlp_up(n):
        h = _dot(x1b, wup_ref[:, n * fc:(n + 1) * fc])
        return jnp.square(jnp.maximum(h, 0.0)).astype(BF16)

    def mlp_down(acc, h, n):
        return acc + _dot(h, wdn_ref[n * fc:(n + 1) * fc, :])

    def in_proj(j):
        return _dot(xpb, wxy_ref[:, 2 * j * cw2:2 * (j + 1) * cw2]) + bxy_ref[:, 2 * j * cw2:2 * (j + 1) * cw2]

    xp = jnp.concatenate([_dot(pm_ref[...], g) for g in _groups(xb)], axis=0)
    xpb = xp.astype(BF16)
    uy = in_proj(0)
    x1 = _layer_norm(z1_scr[...], g1_ref[...], b1_ref[...])
    x1_scr[slot] = x1
    x1b = x1.astype(BF16)
    hs = {}

    def mlp_ops(acc, ops):
        for kind, n in ops:
            if kind == "u":
                hs[n] = mlp_up(n)
            else:
                acc = mlp_down(acc, hs.pop(n), n)
        return acc

    phase_ops = [[("u", 1), ("d", 0), ("u", 2)], [("d", 1), ("u", 3), ("d", 2)],
                 [("u", 4), ("d", 3), ("u", 5)], [("d", 4), ("u", 6), ("d", 5)]]
    acc = mlp_ops(acc, [("u", 0)])
    for j in range(n_col):
        c0 = j * cw2
        acc = mlp_ops(acc, phase_ops[j])
        if j == 0:
            y_ref[...] = _layer_norm(ALPHA * x1_scr[1 - slot] + acc_scr[...], g2_ref[...], b2_ref[...])
        u = uy[:, :cw2]
        y = _gelu_tanh(uy[:, cw2:])
        z = cw_ref[0:1, c0:c0 + cw2] * u
        for k in range(1, CONV_W):
            carry = cc_scr[k - 1:k, c0:c0 + cw2]
            cc_scr[k - 1:k, c0:c0 + cw2] = z[tt - 1:tt, :]
            z = cw_ref[k:k + 1, c0:c0 + cw2] * u + _shift_one_step(z, carry)
        xc = z + cb_ref[:, c0:c0 + cw2]
        ul_scr[:, c0:c0 + cw2] = u[tt - (CONV_W - 1) * SUBLANES:, :]
        xcs = [xc[:, q * LRU_BLOCK:(q + 1) * LRU_BLOCK] for q in range(2)]
        gt2 = _dot(xc.astype(BF16), wg_ref[j])
        gts = [gt2[:, q * cw2:(q + 1) * cw2] for q in range(2)]
        if j + 1 < n_col:
            uy = in_proj(j + 1)
        for q in range(2):
            n = 2 * j + q
            lo = n * LRU_BLOCK
            a, b = _lru_gate_math(gts[q], xcs[q], n, bg_ref, nla)
            h, h_end = _scan_interleaved(a, b, hc_scr[0:1, lo:lo + LRU_BLOCK])
            hc_scr[0:1, lo:lo + LRU_BLOCK] = h_end
            hy_scr[:, lo:lo + LRU_BLOCK] = (h * y[:, q * LRU_BLOCK:(q + 1) * LRU_BLOCK]).astype(BF16)

    acc = mlp_ops(acc, [("u", 7), ("d", 6)])
    hy = jnp.concatenate([_dot(pmt_ref[...], g) for g in _groups(hy_scr[...])], axis=0)
    acc = mlp_ops(acc, [("d", 7)])
    out = _dot(hy.astype(BF16), wo_ref[...]) + bo_ref[...]
    assert not hs and n_mlp == 8 and n_col == len(phase_ops)
    z1_scr[...] = ALPHA * x + out
    acc_scr[...] = acc

    @pl.when(jnp.logical_and(t == tiles_per_seq - 1, s < n_tiles))
    def _():
        hout_ref[...] = hc_scr[0:1, :]
        for k in range(CONV_W - 1):
            r = (k + 1) * SUBLANES - 1
            cout_ref[k:k + 1, :] = ul_scr[r:r + 1, :]


def _lru_sample_kernel(x_ref, h0_ref, cs_ref, wxy_ref, bxy_ref, cw_ref, cb_ref, wg_ref, bg_ref,
                       lam_ref, wo_ref, bo_ref, g_ref, b_ref,
                       y_ref, hout_ref, cout_ref, *, nb, nt):
    x = x_ref[...]
    u, y = _lru_in_proj(x, wxy_ref, bxy_ref)
    upad = jnp.concatenate([cs_ref[...], u], axis=0)
    xc = cb_ref[...]
    for k in range(CONV_W):
        xc = xc + cw_ref[k:k + 1, :] * upad[k * nb:(k + nt) * nb, :]
    nla = _neg_c_softplus_neg(lam_ref[...])
    a_parts, b_parts = [], []
    for n in range(LRU_BLOCKS):
        lo = n * LRU_BLOCK
        a, b = _lru_gate_chunk(xc[:, lo:lo + LRU_BLOCK], n, wg_ref, bg_ref, nla)
        a_parts.append(a)
        b_parts.append(b)
    a = jnp.concatenate(a_parts, axis=1)
    b = jnp.concatenate(b_parts, axis=1)
    h = h0_ref[...]
    hs = []
    for t in range(nt):
        h = a[t * nb:(t + 1) * nb, :] * h + b[t * nb:(t + 1) * nb, :]
        hs.append(h)
    y_ref[...] = _lru_out(jnp.concatenate(hs, axis=0), y, x, wo_ref, bo_ref, g_ref, b_ref)
    hout_ref[...] = h
    cout_ref[...] = upad[nt * nb:, :]


def _lru_weight_specs(paired_gates=False):
    pair = 2 if paired_gates else 1
    return [_const_spec((D_MODEL, 2 * D_RNN)), _const_spec((1, 2 * D_RNN)),
            _const_spec((CONV_W, D_RNN)), _const_spec((1, D_RNN)),
            _const_spec((LRU_BLOCKS // pair, pair * LRU_BLOCK, pair * 2 * LRU_BLOCK)),
            _const_spec((LRU_BLOCKS, 1, 2 * LRU_BLOCK)),
            _const_spec((1, D_RNN)), _const_spec((D_RNN, D_MODEL)), _const_spec((1, D_MODEL)),
            _const_spec((1, D_MODEL)), _const_spec((1, D_MODEL))]


def _layer0_prompt(x, lru_weights, mlp_weights):
    bsz, seq, _ = x.shape
    tt = LRU_TT
    assert seq % tt == 0
    tps = seq // tt
    n_tiles = bsz * tps

    def lru_tile(s):
        s = jnp.minimum(s, n_tiles - 1)
        return (s // tps, s % tps, 0)

    def mlp_tile(s):
        s = jnp.maximum(s - 2, 0)
        return (s // tps, s % tps, 0)

    def state_block(s):
        return (jnp.minimum(s, n_tiles - 1) // tps, 0, 0)

    assert tt % LRU_IL == 0
    pm = _interleave_matrix(LRU_IL)
    return pl.pallas_call(
        functools.partial(_layer0_prompt_kernel, tiles_per_seq=tps),
        grid=(n_tiles + 2,),
        in_specs=[pl.BlockSpec((None, tt, D_MODEL), lru_tile), _const_spec((LRU_IL, LRU_IL)),
                  _const_spec((LRU_IL, LRU_IL))]
        + _lru_weight_specs(paired_gates=True) + _mlp_weight_specs(),
        out_specs=[pl.BlockSpec((None, tt, D_MODEL), mlp_tile),
                   pl.BlockSpec((None, 1, D_RNN), state_block),
                   pl.BlockSpec((None, CONV_W - 1, D_RNN), state_block)],
        out_shape=[jax.ShapeDtypeStruct((bsz, seq, D_MODEL), F32),
                   jax.ShapeDtypeStruct((bsz, 1, D_RNN), F32),
                   jax.ShapeDtypeStruct((bsz, CONV_W - 1, D_RNN), F32)],
        scratch_shapes=[pltpu.VMEM((tt, D_RNN), BF16),
                        pltpu.VMEM((SUBLANES, D_RNN), F32),
                        pltpu.VMEM((SUBLANES, D_RNN), F32),
                        pltpu.VMEM(((CONV_W - 1) * SUBLANES, D_RNN), F32),
                        pltpu.VMEM((2, tt, D_MODEL), F32),
                        pltpu.VMEM((tt, D_MODEL), F32),
                        pltpu.VMEM((tt, D_MODEL), F32)],
        compiler_params=pltpu.CompilerParams(
            dimension_semantics=("arbitrary",), vmem_limit_bytes=VMEM_LIMIT),
        name="layer0_prompt",
    )(x, pm, pm.T, *lru_weights, *mlp_weights)


def _lru_sample(x_tm, h0, cs_tm, weights, nb, nt):
    rows = nb * nt
    return pl.pallas_call(
        functools.partial(_lru_sample_kernel, nb=nb, nt=nt),
        grid=(1,),
        in_specs=[_const_spec((rows, D_MODEL)), _const_spec((nb, D_RNN)),
                  _const_spec(((CONV_W - 1) * nb, D_RNN))] + _lru_weight_specs(),
        out_specs=[_const_spec((rows, D_MODEL)), _const_spec((nb, D_RNN)),
                   _const_spec(((CONV_W - 1) * nb, D_RNN))],
        out_shape=[jax.ShapeDtypeStruct((rows, D_MODEL), F32),
                   jax.ShapeDtypeStruct((nb, D_RNN), F32),
                   jax.ShapeDtypeStruct(((CONV_W - 1) * nb, D_RNN), F32)],
        compiler_params=pltpu.CompilerParams(
            dimension_semantics=("arbitrary",), vmem_limit_bytes=VMEM_LIMIT),
        name="lru_sample",
    )(x_tm, h0, cs_tm, *weights)


def _lane_is_kv0(shape):
    return lax.broadcasted_iota(jnp.int32, shape, len(shape) - 1) < HEAD_DIM


def _sink_softmax(s, mask, sink):
    s = jnp.where(mask, s, -jnp.inf)
    m = jnp.maximum(jnp.max(s, axis=-1, keepdims=True), sink)
    p = jnp.exp(s - m)
    denom = jnp.sum(p, axis=-1, keepdims=True) + jnp.exp(sink - m)
    return p.astype(BF16), 1.0 / denom


def _layer1_prompt_kernel(sink_ref, x_ref, wqkv_ref, bqkv_ref, wo_ref, bo_ref, g1_ref, b1_ref,
                          wup_ref, wdn_ref, g2_ref, b2_ref,
                          y_ref, kout_ref, vout_ref,
                          q_scr, k0_scr, k1_scr, v_scr, o_scr, kl_scr, vl_scr, x1_scr, z1_scr, acc_scr,
                          *, tiles_per_seq):
    tt = SWA_TT
    blk = WINDOW
    s_id = pl.program_id(0)
    n_tiles = pl.num_programs(0) - 2
    t = s_id % tiles_per_seq
    slot = s_id % 2

    @pl.when(s_id == 0)
    def _():
        x1_scr[1] = jnp.zeros((tt, D_MODEL), F32)
        z1_scr[...] = jnp.zeros((tt, D_MODEL), F32)
        acc_scr[...] = jnp.zeros((tt, D_MODEL), F32)

    @pl.when(t == 0)
    def _():
        zeros = jnp.zeros((blk, KV_DIM), BF16)
        k0_scr[0:blk, :] = zeros
        k1_scr[0:blk, :] = zeros
        v_scr[0:blk, :] = zeros

    x = x_ref[...]
    xb = x.astype(BF16)
    acc = jnp.zeros((tt, D_MODEL), F32)
    n_mlp = L1_MLP_CHUNKS
    fc = D_FF // n_mlp
    hs = {}
    nxt_up = [0]
    nxt_dn = [0]

    def mlp_up():
        n = nxt_up[0]
        nxt_up[0] += 1
        h = _dot(x1b, wup_ref[:, n * fc:(n + 1) * fc])
        hs[n] = jnp.square(jnp.maximum(h, 0.0)).astype(BF16)

    def mlp_down(acc):
        n = nxt_dn[0]
        nxt_dn[0] += 1
        return acc + _dot(hs.pop(n), wdn_ref[n * fc:(n + 1) * fc, :])

    def mlp_fill(acc):
        if nxt_up[0] < n_mlp:
            mlp_up()
        if nxt_dn[0] < nxt_up[0] - 1 or (nxt_up[0] == n_mlp and nxt_dn[0] < n_mlp):
            acc = mlp_down(acc)
        return acc

    qkv =_dot(xb, wqkv_ref[...]) + bqkv_ref[...]
    x1 = _layer_norm(z1_scr[...], g1_ref[...], b1_ref[...])
    x1_scr[slot] = x1
    x1b = x1.astype(BF16)
    mlp_up()
    if n_mlp - 2 * (tt // blk) >= 4:
        acc = mlp_fill(acc)
    y_ref[...] = _layer_norm(ALPHA * x1_scr[1 - slot] + acc_scr[...], g2_ref[...], b2_ref[...])
    k = qkv[:, Q_DIM:Q_DIM + KV_DIM]
    v = qkv[:, Q_DIM + KV_DIM:]
    q_scr[...] = (qkv[:, :Q_DIM] * (HEAD_DIM ** -0.5)).astype(BF16)
    kv0 = _lane_is_kv0((tt, KV_DIM))
    k0_scr[blk:, :] = jnp.where(kv0, k, 0.0).astype(BF16)
    k1_scr[blk:, :] = jnp.where(kv0, 0.0, k).astype(BF16)
    v_scr[blk:, :] = v.astype(BF16)
    kl_scr[...] = k[tt - blk:, :]
    vl_scr[...] = v[tt - blk:, :]

    qi = lax.broadcasted_iota(jnp.int32, (blk, 2 * blk), 0)
    kj = lax.broadcasted_iota(jnp.int32, (blk, 2 * blk), 1)
    band = (kj > qi) & (kj <= qi + blk)
    kv0_o = _lane_is_kv0((blk, KV_DIM))

    n_blk = tt // blk
    def score_matmul(i):
        r0 = i * blk
        qs = jnp.concatenate(
            [q_scr[r0:r0 + blk, g * KV_DIM:(g + 1) * KV_DIM] for g in range(GROUP)], axis=0)
        kband = jnp.concatenate([k0_scr[r0:r0 + 2 * blk, :], k1_scr[r0:r0 + 2 * blk, :]], axis=0)
        return _dot_nt(qs, kband)

    def softmax_pv(i, scores_i):
        first = jnp.logical_and(t == 0, i == 0)
        mask = band & (kj >= jnp.where(first, blk, 0))
        ps, rs = [], []
        for kh in range(N_KV):
            s_kh = scores_i[:, kh * 2 * blk:(kh + 1) * 2 * blk]
            for g in range(GROUP):
                p, r = _sink_softmax(s_kh[g * blk:(g + 1) * blk, :], mask, sink_ref[kh * GROUP + g])
                ps.append(p)
                rs.append(r)
        return _dot(jnp.concatenate(ps, axis=0), v_scr[i * blk:(i + 2) * blk, :]), rs

    def store_out(i, o, rs):
        r0 = i * blk
        for g in range(GROUP):
            o0 = o[g * blk:(g + 1) * blk, :] * rs[g]
            o1 = o[(GROUP + g) * blk:(GROUP + g + 1) * blk, :] * rs[GROUP + g]
            o_scr[r0:r0 + blk, g * KV_DIM:(g + 1) * KV_DIM] = jnp.where(kv0_o, o0, o1).astype(BF16)

    for i0 in range(0, n_blk, SWA_PAIR):
        pair = range(i0, min(i0 + SWA_PAIR, n_blk))
        scores = {}
        for i in pair:
            scores[i] = score_matmul(i)
            acc = mlp_fill(acc)
        for i in pair:
            o, rs = softmax_pv(i, scores[i])
            acc = mlp_fill(acc)
            store_out(i, o, rs)

    out =_dot(o_scr[...], wo_ref[...]) + bo_ref[...]
    while nxt_dn[0] < n_mlp:
        acc = mlp_fill(acc)
    assert not hs and nxt_up[0] == n_mlp
    z1_scr[...] = ALPHA * x + out
    acc_scr[...] = acc

    k0_scr[0:blk, :] = k0_scr[tt:tt + blk, :]
    k1_scr[0:blk, :] = k1_scr[tt:tt + blk, :]
    v_scr[0:blk, :] = v_scr[tt:tt + blk, :]

    @pl.when(jnp.logical_and(t == tiles_per_seq - 1, s_id < n_tiles))
    def _():
        kout_ref[...] = kl_scr[...]
        vout_ref[...] = vl_scr[...]


def _swa_weight_specs():
    return [_const_spec((D_MODEL, QKV_DIM)), _const_spec((1, QKV_DIM)),
            _const_spec((Q_DIM, D_MODEL)), _const_spec((1, D_MODEL)),
            _const_spec((1, D_MODEL)), _const_spec((1, D_MODEL))]


def _layer1_prompt(x, sinks, swa_weights, mlp_weights):
    bsz, seq, _ = x.shape
    tt = SWA_TT
    assert seq % tt == 0 and seq >= WINDOW and tt % WINDOW == 0
    tps = seq // tt
    n_tiles = bsz * tps

    def swa_tile(s):
        s = jnp.minimum(s, n_tiles - 1)
        return (s // tps, s % tps, 0)

    def mlp_tile(s):
        s = jnp.maximum(s - 2, 0)
        return (s // tps, s % tps, 0)

    kv_spec = pl.BlockSpec((None, WINDOW, KV_DIM), lambda s: (jnp.minimum(s, n_tiles - 1) // tps, 0, 0))
    return pl.pallas_call(
        functools.partial(_layer1_prompt_kernel, tiles_per_seq=tps),
        grid=(n_tiles + 2,),
        in_specs=[pl.BlockSpec(memory_space=pltpu.SMEM), pl.BlockSpec((None, tt, D_MODEL), swa_tile)]
        + _swa_weight_specs() + _mlp_weight_specs(),
        out_specs=[pl.BlockSpec((None, tt, D_MODEL), mlp_tile), kv_spec, kv_spec],
        out_shape=[jax.ShapeDtypeStruct((bsz, seq, D_MODEL), F32),
                   jax.ShapeDtypeStruct((bsz, WINDOW, KV_DIM), F32),
                   jax.ShapeDtypeStruct((bsz, WINDOW, KV_DIM), F32)],
        scratch_shapes=[pltpu.VMEM((tt, Q_DIM), BF16),
                        pltpu.VMEM((tt + WINDOW, KV_DIM), BF16),
                        pltpu.VMEM((tt + WINDOW, KV_DIM), BF16),
                        pltpu.VMEM((tt + WINDOW, KV_DIM), BF16),
                        pltpu.VMEM((tt, Q_DIM), BF16),
                        pltpu.VMEM((WINDOW, KV_DIM), F32),
                        pltpu.VMEM((WINDOW, KV_DIM), F32),
                        pltpu.VMEM((2, tt, D_MODEL), F32),
                        pltpu.VMEM((tt, D_MODEL), F32),
                        pltpu.VMEM((tt, D_MODEL), F32)],
        compiler_params=pltpu.CompilerParams(
            dimension_semantics=("arbitrary",), vmem_limit_bytes=VMEM_LIMIT),
        name="layer1_prompt",
    )(sinks, x, *swa_weights, *mlp_weights)


def _swa_sample_kernel(sink_ref, x_ref, ck_ref, cv_ref, wqkv_ref, bqkv_ref, wo_ref, bo_ref, g_ref, b_ref,
                       y_ref, ks_ref, vs_ref,
                       q_scr, knew_scr, vnew_scr, o_scr, *, nt):
    pt = SMP_PAD_T
    wbuf = ck_ref.shape[1]
    nkeys = 2 * WINDOW
    step = pl.program_id(0)

    @pl.when(step == 0)
    def _():
        qkv = _dot(x_ref[...].astype(BF16), wqkv_ref[...]) + bqkv_ref[...]
        q_scr[...] = qkv[:, :Q_DIM] * (HEAD_DIM ** -0.5)
        knew_scr[...] = qkv[:, Q_DIM:Q_DIM + KV_DIM]
        vnew_scr[...] = qkv[:, Q_DIM + KV_DIM:]

    rows = GROUP * pt
    ri = lax.broadcasted_iota(jnp.int32, (rows, nkeys), 0)
    kj = lax.broadcasted_iota(jnp.int32, (rows, nkeys), 1)
    tq = ri % pt
    in_buf = kj < wbuf
    mask = (in_buf & (kj - wbuf + WINDOW > tq)) | (
        jnp.logical_not(in_buf) & (kj - wbuf <= tq) & (kj - wbuf < nt))
    gi = lax.broadcasted_iota(jnp.int32, (rows, 1), 0) // pt
    sink_cols = []
    for kh in range(N_KV):
        col = jnp.zeros((rows, 1), F32)
        for g in range(GROUP):
            col = jnp.where(gi == g, sink_ref[kh * GROUP + g], col)
        sink_cols.append(col)
    kv0_k = _lane_is_kv0((nkeys, KV_DIM))
    kv0_o = _lane_is_kv0((rows, KV_DIM))

    pad_rows = jnp.zeros((nkeys - wbuf - pt, KV_DIM), F32)
    tail_is_old = lax.broadcasted_iota(jnp.int32, (pt, KV_DIM), 0) < pt - nt

    def per_group(gi, carry):
        bs = [gi * SMP_GROUP + i for i in range(SMP_GROUP)]
        staged = []
        for b in bs:
            r0 = pl.multiple_of((step * SMP_BB + b) * pt, pt)
            k_new = knew_scr[pl.ds(r0, pt), :]
            v_new = vnew_scr[pl.ds(r0, pt), :]
            kb = jnp.concatenate([ck_ref[b], k_new, pad_rows], axis=0)
            vb = jnp.concatenate([cv_ref[b], v_new, pad_rows], axis=0).astype(BF16)
            qs = jnp.concatenate(
                [q_scr[pl.ds(r0, pt), g * KV_DIM:(g + 1) * KV_DIM] for g in range(GROUP)], axis=0)
            qs = qs.astype(BF16)
            scores = [_dot_nt(qs, jnp.where(kv0_k, kb, 0.0).astype(BF16)),
                      _dot_nt(qs, jnp.where(kv0_k, 0.0, kb).astype(BF16))]
            staged.append((b, r0, k_new, v_new, vb, scores))
        soft = [[_sink_softmax(sc[kh], mask, sink_cols[kh]) for kh in range(N_KV)]
                for (_, _, _, _, _, sc) in staged]
        for (b, r0, k_new, v_new, vb, _), pr in zip(staged, soft):
            o0 = _dot(pr[0][0], vb) * pr[0][1]
            o1 = _dot(pr[1][0], vb) * pr[1][1]
            o = jnp.where(kv0_o, o0, o1)
            for g in range(GROUP):
                o_scr[pl.ds(r0, pt), g * KV_DIM:(g + 1) * KV_DIM] = o[g * pt:(g + 1) * pt, :]
            for new, c_ref, out_ref in ((k_new, ck_ref, ks_ref), (v_new, cv_ref, vs_ref)):
                out_ref[b, 0:wbuf - pt, :] = c_ref[b, nt:wbuf - pt + nt, :]
                old_tail = pltpu.roll(c_ref[b, wbuf - pt:wbuf, :], pt - nt, axis=0)
                out_ref[b, wbuf - pt:wbuf, :] = jnp.where(
                    tail_is_old, old_tail, pltpu.roll(new, pt - nt, axis=0))
        return carry

    lax.fori_loop(0, SMP_BB // SMP_GROUP, per_group, 0)

    @pl.when(step == pl.num_programs(0) - 1)
    def _():
        out = _dot(o_scr[...].astype(BF16), wo_ref[...]) + bo_ref[...]
        y_ref[...] = _layer_norm(ALPHA * x_ref[...] + out, g_ref[...], b_ref[...])


def _swa_sample(x_pad, cache_k, cache_v, sinks, weights, nt):
    rows = x_pad.shape[0]
    nb, wbuf, _ = cache_k.shape
    assert nb % SMP_BB == 0 and rows == nb * SMP_PAD_T and nt <= SMP_PAD_T
    assert wbuf + SMP_PAD_T <= 2 * WINDOW and wbuf % SMP_PAD_T == 0
    cache_spec = pl.BlockSpec((SMP_BB, wbuf, KV_DIM), lambda i: (i, 0, 0))
    return pl.pallas_call(
        functools.partial(_swa_sample_kernel, nt=nt),
        grid=(nb // SMP_BB,),
        in_specs=[pl.BlockSpec(memory_space=pltpu.SMEM), _const_spec((rows, D_MODEL)),
                  cache_spec, cache_spec] + _swa_weight_specs(),
        out_specs=[_const_spec((rows, D_MODEL)), cache_spec, cache_spec],
        out_shape=[jax.ShapeDtypeStruct((rows, D_MODEL), F32),
                   jax.ShapeDtypeStruct((nb, wbuf, KV_DIM), F32),
                   jax.ShapeDtypeStruct((nb, wbuf, KV_DIM), F32)],
        scratch_shapes=[pltpu.VMEM((rows, Q_DIM), F32),
                        pltpu.VMEM((rows, KV_DIM), F32),
                        pltpu.VMEM((rows, KV_DIM), F32),
                        pltpu.VMEM((rows, Q_DIM), F32)],
        compiler_params=pltpu.CompilerParams(
            dimension_semantics=("arbitrary",), vmem_limit_bytes=VMEM_LIMIT),
        name="swa_sample",
    )(sinks, x_pad, cache_k, cache_v, *weights)


def _row(v):
    return v.reshape(1, -1).astype(F32)


def _pack_lru_weights(j, lru_w_x, lru_b_x, lru_w_y, lru_b_y, lru_conv_w, lru_conv_b,
                      lru_w_ga, lru_b_ga, lru_w_gi, lru_b_gi, lru_lam, lru_w_out, lru_b_out, g, b):
    def chunk_pack(ax, ay):
        lead = ax.shape[:-1]
        cw2 = 2 * LRU_BLOCK
        return jnp.concatenate([ax.reshape(*lead, -1, cw2), ay.reshape(*lead, -1, cw2)],
                               axis=-1).reshape(*lead, 2 * D_RNN)

    wxy = chunk_pack(lru_w_x[j].astype(BF16), lru_w_y[j].astype(BF16))
    bxy = chunk_pack(lru_b_x[j], lru_b_y[j]).reshape(1, -1)
    wg = jnp.concatenate([lru_w_ga[j], lru_w_gi[j]], axis=2).astype(BF16)
    bg = jnp.concatenate([lru_b_ga[j], lru_b_gi[j]], axis=1)[:, None, :]
    return (wxy, bxy, lru_conv_w[j], _row(lru_conv_b[j]), wg, bg, _row(lru_lam[j]),
            lru_w_out[j].astype(BF16), _row(lru_b_out[j]), _row(g), _row(b))


def _pack_swa_weights(j, swa_w_qkv, swa_b_qkv, swa_w_o, swa_b_o, g, b):
    w, bias = swa_w_qkv[j].astype(BF16), swa_b_qkv[j]
    wq = w[:, :Q_DIM].reshape(D_MODEL, N_KV, GROUP, HEAD_DIM).transpose(0, 2, 1, 3).reshape(D_MODEL, Q_DIM)
    bq = bias[:Q_DIM].reshape(N_KV, GROUP, HEAD_DIM).transpose(1, 0, 2).reshape(Q_DIM)
    wqkv = jnp.concatenate([wq, w[:, Q_DIM:]], axis=1)
    bqkv = jnp.concatenate([bq, bias[Q_DIM:]]).reshape(1, -1)
    wo = swa_w_o[j].astype(BF16).reshape(N_KV, GROUP, HEAD_DIM, D_MODEL).transpose(1, 0, 2, 3)
    return (wqkv, bqkv, wo.reshape(Q_DIM, D_MODEL), _row(swa_b_o[j]), _row(g), _row(b))


def kernel(x_prompt, x_sample, state_lru_h, state_lru_conv, cache_swa_k, cache_swa_v,
           lru_w_x, lru_b_x, lru_w_y, lru_b_y, lru_conv_w, lru_conv_b,
           lru_w_ga, lru_b_ga, lru_w_gi, lru_b_gi, lru_lam, lru_w_out, lru_b_out,
           swa_w_qkv, swa_b_qkv, swa_sinks, swa_w_o, swa_b_o,
           mlp_w_up, mlp_w_down, ln1_g, ln1_b, ln2_g, ln2_b):
    bsz, seq, _ = x_prompt.shape
    nb, nt, _ = x_sample.shape
    wbuf = cache_swa_k.shape[2]

    mlp_w = [(mlp_w_up[i].astype(BF16), mlp_w_down[i].astype(BF16), _row(ln2_g[i]), _row(ln2_b[i]))
             for i in range(DEPTH)]

    def mlp(x2d, i):
        return _mlp_block(x2d, *mlp_w[i])

    lw = _pack_lru_weights(0, lru_w_x, lru_b_x, lru_w_y, lru_b_y, lru_conv_w, lru_conv_b,
                           lru_w_ga, lru_b_ga, lru_w_gi, lru_b_gi, lru_lam, lru_w_out, lru_b_out,
                           ln1_g[0], ln1_b[0])
    wg = lw[4]
    zero = jnp.zeros_like(wg[0::2])
    wg_pairs = jnp.concatenate([jnp.concatenate([wg[0::2], zero], axis=2),
                                jnp.concatenate([zero, wg[1::2]], axis=2)], axis=1)
    lw_p = lw[:4] + (wg_pairs,) + lw[5:]
    xp, h_p, conv_p = _layer0_prompt(x_prompt, lw_p, mlp_w[0])

    xs_tm = x_sample.transpose(1, 0, 2).reshape(nt * nb, D_MODEL)
    cs_tm = state_lru_conv[0].transpose(1, 0, 2).reshape((CONV_W - 1) * nb, D_RNN)
    xs_tm, h_s, conv_s_tm = _lru_sample(xs_tm, state_lru_h[0], cs_tm, lw, nb, nt)
    xs_tm = mlp(xs_tm, 0)
    conv_s = conv_s_tm.reshape(CONV_W - 1, nb, D_RNN).transpose(1, 0, 2)

    sw = _pack_swa_weights(0, swa_w_qkv, swa_b_qkv, swa_w_o, swa_b_o, ln1_g[1], ln1_b[1])
    sinks = swa_sinks[0].astype(F32)
    xp, k_p, v_p = _layer1_prompt(xp, sinks, sw, mlp_w[1])

    xs_bm = xs_tm.reshape(nt, nb, D_MODEL).transpose(1, 0, 2)
    xs_pad = jnp.pad(xs_bm, ((0, 0), (0, SMP_PAD_T - nt), (0, 0))).reshape(nb * SMP_PAD_T, D_MODEL)
    ck = cache_swa_k[0].reshape(nb, wbuf, KV_DIM)
    cv = cache_swa_v[0].reshape(nb, wbuf, KV_DIM)
    ys_pad, k_s, v_s = _swa_sample(xs_pad, ck, cv, sinks, sw, nt)
    xs = ys_pad.reshape(nb, SMP_PAD_T, D_MODEL)[:, :nt].reshape(nb * nt, D_MODEL)
    xs = mlp(xs, 1).reshape(nb, nt, D_MODEL)

    kv_shape = (1, -1, min(WINDOW, seq), N_KV, HEAD_DIM)
    return (xp, xs,
            h_p.reshape(1, bsz, D_RNN), conv_p[None],
            k_p.reshape(kv_shape), v_p.reshape(kv_shape),
            h_s[None], conv_s[None],
            k_s.reshape(1, nb, wbuf, N_KV, HEAD_DIM), v_s.reshape(1, nb, wbuf, N_KV, HEAD_DIM))
```

```python
import functools

import jax
import jax.numpy as jnp
from jax import lax
from jax.experimental import pallas as pl
from jax.experimental.pallas import tpu as pltpu

F32 = jnp.float32
BF16 = jnp.bfloat16

D_MODEL = 1024
D_RNN = 1024
D_FF = 4096
LRU_BLOCKS = 8
LRU_BLOCK = 128
CONV_W = 4
LRU_C = 8.0
N_HEADS = 16
HEAD_DIM = 64
N_KV = 2
GROUP = 8
WINDOW = 128
KV_DIM = N_KV * HEAD_DIM
Q_DIM = N_HEADS * HEAD_DIM
QKV_DIM = Q_DIM + 2 * KV_DIM
DEPTH = 2
ALPHA = (2.0 * DEPTH) ** 0.25
LN_EPS = 1e-5

SUBLANES = 8
VMEM_LIMIT = 60 * 1024 * 1024

MLP_TM = 512
MLP_FC = 1024
LRU_TT = 512
LRU_IL = 256
SWA_TT = 512
SWA_PAIR = 2
L1_MLP_CHUNKS = 8
SMP_PAD_T = 8
SMP_BB = 16
SMP_GROUP = 4


def _const_spec(shape):
    nd = len(shape)
    return pl.BlockSpec(shape, lambda *_: (0,) * nd, pipeline_mode=pl.Buffered(1))


def _dot(a, b):
    return jnp.dot(a, b, preferred_element_type=F32)


def _dot_nt(a, b):
    return lax.dot_general(a, b, (((1,), (1,)), ((), ())), preferred_element_type=F32)


def _layer_norm(z, g, b):
    mu = jnp.mean(z, axis=-1, keepdims=True)
    d = z - mu
    var = jnp.mean(d * d, axis=-1, keepdims=True)
    return d * lax.rsqrt(var + LN_EPS) * g + b


def _sigmoid(x):
    return 0.5 * jnp.tanh(0.5 * x) + 0.5


def _mlp_acc(x, wup_ref, wdn_ref):
    xb = x.astype(BF16)
    acc = jnp.zeros(x.shape, F32)
    for j in range(D_FF // MLP_FC):
        h = _dot(xb, wup_ref[:, j * MLP_FC:(j + 1) * MLP_FC])
        h = jnp.square(jnp.maximum(h, 0.0)).astype(BF16)
        acc = acc + _dot(h, wdn_ref[j * MLP_FC:(j + 1) * MLP_FC, :])
    return acc


def _mlp_ln(x, wup_ref, wdn_ref, g_ref, b_ref):
    return _layer_norm(ALPHA * x + _mlp_acc(x, wup_ref, wdn_ref), g_ref[...], b_ref[...])


def _mlp_kernel(x_ref, wup_ref, wdn_ref, g_ref, b_ref, o_ref):
    o_ref[...] = _mlp_ln(x_ref[...], wup_ref, wdn_ref, g_ref, b_ref)


def _mlp_weight_specs():
    return [_const_spec((D_MODEL, D_FF)), _const_spec((D_FF, D_MODEL)),
            _const_spec((1, D_MODEL)), _const_spec((1, D_MODEL))]


def _mlp_block(x, w_up, w_dn, g, b):
    n = x.shape[0]
    tm = min(MLP_TM, n)
    assert n % tm == 0
    row_spec = pl.BlockSpec((tm, D_MODEL), lambda i: (i, 0))
    return pl.pallas_call(
        _mlp_kernel,
        grid=(n // tm,),
        in_specs=[row_spec] + _mlp_weight_specs(),
        out_specs=row_spec,
        out_shape=jax.ShapeDtypeStruct((n, D_MODEL), F32),
        compiler_params=pltpu.CompilerParams(
            dimension_semantics=("arbitrary",), vmem_limit_bytes=VMEM_LIMIT),
        name="mlp_ln",
    )(x, w_up, w_dn, g, b)


_GELU_C0 = (2.0 / jnp.pi) ** 0.5
_GELU_C1 = _GELU_C0 * 0.044715


def _gelu_tanh(x):
    t = jnp.tanh(x * (_GELU_C0 + _GELU_C1 * (x * x)))
    hx = 0.5 * x
    return hx + hx * t


def _lru_in_proj(x, wxy_ref, bxy_ref):
    uy = _dot(x.astype(BF16), wxy_ref[...]) + bxy_ref[...]
    cw2 = 2 * LRU_BLOCK
    n_col = D_RNN // cw2
    u = jnp.concatenate([uy[:, 2 * j * cw2:(2 * j + 1) * cw2] for j in range(n_col)], axis=1)
    y_pre = jnp.concatenate([uy[:, (2 * j + 1) * cw2:(2 * j + 2) * cw2] for j in range(n_col)], axis=1)
    return u, _gelu_tanh(y_pre)


def _neg_c_softplus_neg(lam):
    z = -lam
    sp = jnp.maximum(z, 0.0) + jnp.log1p(jnp.exp(-jnp.abs(z)))
    return -LRU_C * sp


def _lru_gate_chunk(xc_n, n, wg_ref, bg_ref, nla):
    return _lru_gate_math(_dot(xc_n.astype(BF16), wg_ref[n]), xc_n, n, bg_ref, nla)


def _lru_gate_math(gt, xc_n, n, bg_ref, nla):
    lo = n * LRU_BLOCK
    gt = gt + bg_ref[n]
    r = _sigmoid(gt[:, :LRU_BLOCK])
    gi = _sigmoid(gt[:, LRU_BLOCK:])
    log_a = r * nla[:, lo:lo + LRU_BLOCK]
    a = jnp.exp(log_a)
    v = -jnp.tanh(log_a) * (a * a + 1.0)
    mult = jnp.where(v > 0.0, v * lax.rsqrt(v), 0.0)
    return a, mult * (gi * xc_n)


def _lru_out(h, y, x, wo_ref, bo_ref, g_ref, b_ref):
    out = _dot((h * y).astype(BF16), wo_ref[...]) + bo_ref[...]
    return _layer_norm(ALPHA * x + out, g_ref[...], b_ref[...])


def _interleave_matrix(tt):
    r = jnp.arange(tt)
    src_time = (r % SUBLANES) * (tt // SUBLANES) + r // SUBLANES
    return (src_time[:, None] == jnp.arange(tt)[None, :]).astype(BF16)


def _groups(z):
    return [z[r:r + LRU_IL, :] for r in range(0, z.shape[0], LRU_IL)]


def _shift_one_step(z, carry_row):
    row = lax.broadcasted_iota(jnp.int32, (SUBLANES, z.shape[1]), 0)
    out = []
    for zg in _groups(z):
        last = zg[LRU_IL - SUBLANES:, :]
        out.append(jnp.where(row == 0, carry_row, pltpu.roll(last, 1, axis=0)))
        out.append(zg[:LRU_IL - SUBLANES, :])
        carry_row = zg[LRU_IL - 1:, :]
    return jnp.concatenate(out, axis=0)


def _scan_interleaved(a, b, h_in):
    parts = []
    for ag, bg in zip(_groups(a), _groups(b)):
        hg, h_in = _scan_group(ag, bg, h_in)
        parts.append(hg)
    return jnp.concatenate(parts, axis=0), h_in


def _scan_group(a, b, h_in):
    n_steps = a.shape[0] // SUBLANES
    h = b[0:SUBLANES, :]
    p = a[0:SUBLANES, :]
    hs, ps = [h], [p]
    for i in range(1, n_steps):
        ai = a[i * SUBLANES:(i + 1) * SUBLANES, :]
        h = ai * h + b[i * SUBLANES:(i + 1) * SUBLANES, :]
        p = ai * p
        hs.append(h)
        ps.append(p)
    row = lax.broadcasted_iota(jnp.int32, h.shape, 0)
    cin = h_in
    cvec = jnp.zeros(h.shape, F32)
    for c in range(SUBLANES):
        cvec = jnp.where(row == c, cin, cvec)
        cin = h[c:c + 1, :] + p[c:c + 1, :] * cin
    full = jnp.concatenate([hs[i] + ps[i] * cvec for i in range(n_steps)], axis=0)
    return full, cin


def _drain_steps(s, n_tiles, g1_ref, b1_ref, wup_ref, wdn_ref, g2_ref, b2_ref, y_ref, x1_scr, z1_scr, acc_scr):
    slot = s % 2

    @pl.when(s == n_tiles)
    def _():
        x1 = _layer_norm(z1_scr[...], g1_ref[...], b1_ref[...])
        x1_scr[slot] = x1
        y_ref[...] = _layer_norm(ALPHA * x1_scr[1 - slot] + acc_scr[...], g2_ref[...], b2_ref[...])
        acc_scr[...] = _mlp_acc(x1, wup_ref, wdn_ref)

    @pl.when(s == n_tiles + 1)
    def _():
        y_ref[...] = _layer_norm(ALPHA * x1_scr[1 - slot] + acc_scr[...], g2_ref[...], b2_ref[...])


def _layer0_prompt_kernel(*refs, tiles_per_seq):
    g1_ref, b1_ref, wup_ref, wdn_ref, g2_ref, b2_ref, y_ref = refs[12:19]
    x1_scr, z1_scr, acc_scr = refs[-3:]
    s = pl.program_id(0)
    n_tiles = pl.num_programs(0) - 2

    @pl.when(s < n_tiles)
    def _():
        _layer0_tile_step(*refs, tiles_per_seq=tiles_per_seq)

    _drain_steps(s, n_tiles, g1_ref, b1_ref, wup_ref, wdn_ref, g2_ref, b2_ref, y_ref, x1_scr, z1_scr, acc_scr)


def _layer0_tile_step(x_ref, pm_ref, pmt_ref, wxy_ref, bxy_ref, cw_ref, cb_ref, wg_ref, bg_ref, lam_ref,
                      wo_ref, bo_ref, g1_ref, b1_ref, wup_ref, wdn_ref, g2_ref, b2_ref,
                      y_ref, hout_ref, cout_ref,
                      hy_scr, hc_scr, cc_scr, ul_scr, x1_scr, z1_scr, acc_scr, *, tiles_per_seq):
    tt = LRU_TT
    s = pl.program_id(0)
    n_tiles = pl.num_programs(0) - 2
    t = s % tiles_per_seq
    slot = s % 2

    @pl.when(s == 0)
    def _():
        x1_scr[1] = jnp.zeros((tt, D_MODEL), F32)
        z1_scr[...] = jnp.zeros((tt, D_MODEL), F32)
        acc_scr[...] = jnp.zeros((tt, D_MODEL), F32)

    @pl.when(t == 0)
    def _():
        hc_scr[...] = jnp.zeros(hc_scr.shape, F32)
        cc_scr[...] = jnp.zeros(cc_scr.shape, F32)

    x = x_ref[...]
    xb = x.astype(BF16)
    acc = jnp.zeros((tt, D_MODEL), F32)
    nla = _neg_c_softplus_neg(lam_ref[...])
    n_mlp = LRU_BLOCKS
    fc = D_FF // n_mlp
    cw2 = 2 * LRU_BLOCK
    n_col = D_RNN // cw2

    def mlp_up(n):
        h = _dot(x1b, wup_ref[:, n * fc:(n + 1) * fc])
        return jnp.square(jnp.maximum(h, 0.0)).astype(BF16)

    def mlp_down(acc, h, n):
        return acc + _dot(h, wdn_ref[n * fc:(n + 1) * fc, :])

    def in_proj(j):
        return _dot(xpb, wxy_ref[:, 2 * j * cw2:2 * (j + 1) * cw2]) + bxy_ref[:, 2 * j * cw2:2 * (j + 1) * cw2]

    xp = jnp.concatenate([_dot(pm_ref[...], g) for g in _groups(xb)], axis=0)
    xpb = xp.astype(BF16)
    uy = in_proj(0)
    x1 = _layer_norm(z1_scr[...], g1_ref[...], b1_ref[...])
    x1_scr[slot] = x1
    x1b = x1.astype(BF16)
    hs = {}

    def mlp_ops(acc, ops):
        for kind, n in ops:
            if kind == "u":
                hs[n] = mlp_up(n)
            else:
                acc = mlp_down(acc, hs.pop(n), n)
        return acc

    phase_ops = [[("u", 1), ("d", 0), ("u", 2)], [("d", 1), ("u", 3), ("d", 2)],
                 [("u", 4), ("d", 3), ("u", 5)], [("d", 4), ("u", 6), ("d", 5)]]
    acc = mlp_ops(acc, [("u", 0)])
    for j in range(n_col):
        c0 = j * cw2
        acc = mlp_ops(acc, phase_ops[j])
        if j == 0:
            y_ref[...] = _layer_norm(ALPHA * x1_scr[1 - slot] + acc_scr[...], g2_ref[...], b2_ref[...])
        u = uy[:, :cw2]
        y = _gelu_tanh(uy[:, cw2:])
        z = cw_ref[0:1, c0:c0 + cw2] * u
        for k in range(1, CONV_W):
            carry = cc_scr[k - 1:k, c0:c0 + cw2]
            cc_scr[k - 1:k, c0:c0 + cw2] = z[tt - 1:tt, :]
            z = cw_ref[k:k + 1, c0:c0 + cw2] * u + _shift_one_step(z, carry)
        xc = z + cb_ref[:, c0:c0 + cw2]
        ul_scr[:, c0:c0 + cw2] = u[tt - (CONV_W - 1) * SUBLANES:, :]
        xcs = [xc[:, q * LRU_BLOCK:(q + 1) * LRU_BLOCK] for q in range(2)]
        gt2 = _dot(xc.astype(BF16), wg_ref[j])
        gts = [gt2[:, q * cw2:(q + 1) * cw2] for q in range(2)]
        if j + 1 < n_col:
            uy = in_proj(j + 1)
        for q in range(2):
            n = 2 * j + q
            lo = n * LRU_BLOCK
            a, b = _lru_gate_math(gts[q], xcs[q], n, bg_ref, nla)
            h, h_end = _scan_interleaved(a, b, hc_scr[0:1, lo:lo + LRU_BLOCK])
            hc_scr[0:1, lo:lo + LRU_BLOCK] = h_end
            hy_scr[:, lo:lo + LRU_BLOCK] = (h * y[:, q * LRU_BLOCK:(q + 1) * LRU_BLOCK]).astype(BF16)

    acc = mlp_ops(acc, [("u", 7), ("d", 6)])
    hy = jnp.concatenate([_dot(pmt_ref[...], g) for g in _groups(hy_scr[...])], axis=0)
    acc = mlp_ops(acc, [("d", 7)])
    out = _dot(hy.astype(BF16), wo_ref[...]) + bo_ref[...]
    assert not hs and n_mlp == 8 and n_col == len(phase_ops)
    z1_scr[...] = ALPHA * x + out
    acc_scr[...] = acc

    @pl.when(jnp.logical_and(t == tiles_per_seq - 1, s < n_tiles))
    def _():
        hout_ref[...] = hc_scr[0:1, :]
        for k in range(CONV_W - 1):
            r = (k + 1) * SUBLANES - 1
            cout_ref[k:k + 1, :] = ul_scr[r:r + 1, :]


def _lru_sample_kernel(x_ref, h0_ref, cs_ref, wxy_ref, bxy_ref, cw_ref, cb_ref, wg_ref, bg_ref,
                       lam_ref, wo_ref, bo_ref, g_ref, b_ref,
                       y_ref, hout_ref, cout_ref, *, nb, nt):
    x = x_ref[...]
    u, y = _lru_in_proj(x, wxy_ref, bxy_ref)
    upad = jnp.concatenate([cs_ref[...], u], axis=0)
    xc = cb_ref[...]
    for k in range(CONV_W):
        xc = xc + cw_ref[k:k + 1, :] * upad[k * nb:(k + nt) * nb, :]
    nla = _neg_c_softplus_neg(lam_ref[...])
    a_parts, b_parts = [], []
    for n in range(LRU_BLOCKS):
        lo = n * LRU_BLOCK
        a, b = _lru_gate_chunk(xc[:, lo:lo + LRU_BLOCK], n, wg_ref, bg_ref, nla)
        a_parts.append(a)
        b_parts.append(b)
    a = jnp.concatenate(a_parts, axis=1)
    b = jnp.concatenate(b_parts, axis=1)
    h = h0_ref[...]
    hs = []
    for t in range(nt):
        h = a[t * nb:(t + 1) * nb, :] * h + b[t * nb:(t + 1) * nb, :]
        hs.append(h)
    y_ref[...] = _lru_out(jnp.concatenate(hs, axis=0), y, x, wo_ref, bo_ref, g_ref, b_ref)
    hout_ref[...] = h
    cout_ref[...] = upad[nt * nb:, :]


def _lru_weight_specs(paired_gates=False):
    pair = 2 if paired_gates else 1
    return [_const_spec((D_MODEL, 2 * D_RNN)), _const_spec((1, 2 * D_RNN)),
            _const_spec((CONV_W, D_RNN)), _const_spec((1, D_RNN)),
            _const_spec((LRU_BLOCKS // pair, pair * LRU_BLOCK, pair * 2 * LRU_BLOCK)),
            _const_spec((LRU_BLOCKS, 1, 2 * LRU_BLOCK)),
            _const_spec((1, D_RNN)), _const_spec((D_RNN, D_MODEL)), _const_spec((1, D_MODEL)),
            _const_spec((1, D_MODEL)), _const_spec((1, D_MODEL))]


def _layer0_prompt(x, lru_weights, mlp_weights):
    bsz, seq, _ = x.shape
    tt = LRU_TT
    assert seq % tt == 0
    tps = seq // tt
    n_tiles = bsz * tps

    def lru_tile(s):
        s = jnp.minimum(s, n_tiles - 1)
        return (s // tps, s % tps, 0)

    def mlp_tile(s):
        s = jnp.maximum(s - 2, 0)
        return (s // tps, s % tps, 0)

    def state_block(s):
        return (jnp.minimum(s, n_tiles - 1) // tps, 0, 0)

    assert tt % LRU_IL == 0
    pm = _interleave_matrix(LRU_IL)
    return pl.pallas_call(
        functools.partial(_layer0_prompt_kernel, tiles_per_seq=tps),
        grid=(n_tiles + 2,),
        in_specs=[pl.BlockSpec((None, tt, D_MODEL), lru_tile), _const_spec((LRU_IL, LRU_IL)),
                  _const_spec((LRU_IL, LRU_IL))]
        + _lru_weight_specs(paired_gates=True) + _mlp_weight_specs(),
        out_specs=[pl.BlockSpec((None, tt, D_MODEL), mlp_tile),
                   pl.BlockSpec((None, 1, D_RNN), state_block),
                   pl.BlockSpec((None, CONV_W - 1, D_RNN), state_block)],
        out_shape=[jax.ShapeDtypeStruct((bsz, seq, D_MODEL), F32),
                   jax.ShapeDtypeStruct((bsz, 1, D_RNN), F32),
                   jax.ShapeDtypeStruct((bsz, CONV_W - 1, D_RNN), F32)],
        scratch_shapes=[pltpu.VMEM((tt, D_RNN), BF16),
                        pltpu.VMEM((SUBLANES, D_RNN), F32),
                        pltpu.VMEM((SUBLANES, D_RNN), F32),
                        pltpu.VMEM(((CONV_W - 1) * SUBLANES, D_RNN), F32),
                        pltpu.VMEM((2, tt, D_MODEL), F32),
                        pltpu.VMEM((tt, D_MODEL), F32),
                        pltpu.VMEM((tt, D_MODEL), F32)],
        compiler_params=pltpu.CompilerParams(
            dimension_semantics=("arbitrary",), vmem_limit_bytes=VMEM_LIMIT),
        name="layer0_prompt",
    )(x, pm, pm.T, *lru_weights, *mlp_weights)


def _lru_sample(x_tm, h0, cs_tm, weights, nb, nt):
    rows = nb * nt
    return pl.pallas_call(
        functools.partial(_lru_sample_kernel, nb=nb, nt=nt),
        grid=(1,),
        in_specs=[_const_spec((rows, D_MODEL)), _const_spec((nb, D_RNN)),
                  _const_spec(((CONV_W - 1) * nb, D_RNN))] + _lru_weight_specs(),
        out_specs=[_const_spec((rows, D_MODEL)), _const_spec((nb, D_RNN)),
                   _const_spec(((CONV_W - 1) * nb, D_RNN))],
        out_shape=[jax.ShapeDtypeStruct((rows, D_MODEL), F32),
                   jax.ShapeDtypeStruct((nb, D_RNN), F32),
                   jax.ShapeDtypeStruct(((CONV_W - 1) * nb, D_RNN), F32)],
        compiler_params=pltpu.CompilerParams(
            dimension_semantics=("arbitrary",), vmem_limit_bytes=VMEM_LIMIT),
        name="lru_sample",
    )(x_tm, h0, cs_tm, *weights)


def _lane_is_kv0(shape):
    return lax.broadcasted_iota(jnp.int32, shape, len(shape) - 1) < HEAD_DIM


def _sink_softmax(s, mask, sink):
    s = jnp.where(mask, s, -jnp.inf)
    m = jnp.maximum(jnp.max(s, axis=-1, keepdims=True), sink)
    p = jnp.exp(s - m)
    denom = jnp.sum(p, axis=-1, keepdims=True) + jnp.exp(sink - m)
    return p.astype(BF16), 1.0 / denom


def _layer1_prompt_kernel(*refs, tiles_per_seq):
    g1_ref, b1_ref, wup_ref, wdn_ref, g2_ref, b2_ref, y_ref = refs[6:13]
    x1_scr, z1_scr, acc_scr = refs[-3:]
    s = pl.program_id(0)
    n_tiles = pl.num_programs(0) - 2

    @pl.when(s < n_tiles)
    def _():
        _layer1_tile_step(*refs, tiles_per_seq=tiles_per_seq)

    _drain_steps(s, n_tiles, g1_ref, b1_ref, wup_ref, wdn_ref, g2_ref, b2_ref, y_ref, x1_scr, z1_scr, acc_scr)


def _layer1_tile_step(sink_ref, x_ref, wqkv_ref, bqkv_ref, wo_ref, bo_ref, g1_ref, b1_ref,
                      wup_ref, wdn_ref, g2_ref, b2_ref,
                      y_ref, kout_ref, vout_ref,
                      q_scr, k0_scr, k1_scr, v_scr, o_scr, kl_scr, vl_scr, x1_scr, z1_scr, acc_scr,
                      *, tiles_per_seq):
    tt = SWA_TT
    blk = WINDOW
    s_id = pl.program_id(0)
    n_tiles = pl.num_programs(0) - 2
    t = s_id % tiles_per_seq
    slot = s_id % 2

    @pl.when(s_id == 0)
    def _():
        x1_scr[1] = jnp.zeros((tt, D_MODEL), F32)
        z1_scr[...] = jnp.zeros((tt, D_MODEL), F32)
        acc_scr[...] = jnp.zeros((tt, D_MODEL), F32)

    @pl.when(t == 0)
    def _():
        zeros = jnp.zeros((blk, KV_DIM), BF16)
        k0_scr[0:blk, :] = zeros
        k1_scr[0:blk, :] = zeros
        v_scr[0:blk, :] = zeros

    x = x_ref[...]
    xb = x.astype(BF16)
    acc = jnp.zeros((tt, D_MODEL), F32)
    n_mlp = L1_MLP_CHUNKS
    fc = D_FF // n_mlp
    hs = {}
    nxt_up = [0]
    nxt_dn = [0]

    def mlp_up():
        n = nxt_up[0]
        nxt_up[0] += 1
        h = _dot(x1b, wup_ref[:, n * fc:(n + 1) * fc])
        hs[n] = jnp.square(jnp.maximum(h, 0.0)).astype(BF16)

    def mlp_down(acc):
        n = nxt_dn[0]
        nxt_dn[0] += 1
        return acc + _dot(hs.pop(n), wdn_ref[n * fc:(n + 1) * fc, :])

    def mlp_fill(acc):
        if nxt_up[0] < n_mlp:
            mlp_up()
        if nxt_dn[0] < nxt_up[0] - 1 or (nxt_up[0] == n_mlp and nxt_dn[0] < n_mlp):
            acc = mlp_down(acc)
        return acc

    qkv =_dot(xb, wqkv_ref[...]) + bqkv_ref[...]
    x1 = _layer_norm(z1_scr[...], g1_ref[...], b1_ref[...])
    x1_scr[slot] = x1
    x1b = x1.astype(BF16)
    mlp_up()
    if n_mlp - 2 * (tt // blk) >= 4:
        acc = mlp_fill(acc)
    y_ref[...] = _layer_norm(ALPHA * x1_scr[1 - slot] + acc_scr[...], g2_ref[...], b2_ref[...])
    k = qkv[:, Q_DIM:Q_DIM + KV_DIM]
    v = qkv[:, Q_DIM + KV_DIM:]
    q_scr[...] = (qkv[:, :Q_DIM] * (HEAD_DIM ** -0.5)).astype(BF16)
    kv0 = _lane_is_kv0((tt, KV_DIM))
    k0_scr[blk:, :] = jnp.where(kv0, k, 0.0).astype(BF16)
    k1_scr[blk:, :] = jnp.where(kv0, 0.0, k).astype(BF16)
    v_scr[blk:, :] = v.astype(BF16)
    kl_scr[...] = k[tt - blk:, :]
    vl_scr[...] = v[tt - blk:, :]

    qi = lax.broadcasted_iota(jnp.int32, (blk, 2 * blk), 0)
    kj = lax.broadcasted_iota(jnp.int32, (blk, 2 * blk), 1)
    band = (kj > qi) & (kj <= qi + blk)
    kv0_o = _lane_is_kv0((blk, KV_DIM))

    n_blk = tt // blk
    def score_matmul(i):
        r0 = i * blk
        qs = jnp.concatenate(
            [q_scr[r0:r0 + blk, g * KV_DIM:(g + 1) * KV_DIM] for g in range(GROUP)], axis=0)
        kband = jnp.concatenate([k0_scr[r0:r0 + 2 * blk, :], k1_scr[r0:r0 + 2 * blk, :]], axis=0)
        return _dot_nt(qs, kband)

    def softmax_pv(i, scores_i):
        first = jnp.logical_and(t == 0, i == 0)
        mask = band & (kj >= jnp.where(first, blk, 0))
        ps, rs = [], []
        for kh in range(N_KV):
            s_kh = scores_i[:, kh * 2 * blk:(kh + 1) * 2 * blk]
            for g in range(GROUP):
                p, r = _sink_softmax(s_kh[g * blk:(g + 1) * blk, :], mask, sink_ref[kh * GROUP + g])
                ps.append(p)
                rs.append(r)
        return _dot(jnp.concatenate(ps, axis=0), v_scr[i * blk:(i + 2) * blk, :]), rs

    def store_out(i, o, rs):
        r0 = i * blk
        for g in range(GROUP):
            o0 = o[g * blk:(g + 1) * blk, :] * rs[g]
            o1 = o[(GROUP + g) * blk:(GROUP + g + 1) * blk, :] * rs[GROUP + g]
            o_scr[r0:r0 + blk, g * KV_DIM:(g + 1) * KV_DIM] = jnp.where(kv0_o, o0, o1).astype(BF16)

    for i0 in range(0, n_blk, SWA_PAIR):
        pair = range(i0, min(i0 + SWA_PAIR, n_blk))
        scores = {}
        for i in pair:
            scores[i] = score_matmul(i)
            acc = mlp_fill(acc)
        for i in pair:
            o, rs = softmax_pv(i, scores[i])
            acc = mlp_fill(acc)
            store_out(i, o, rs)

    out =_dot(o_scr[...], wo_ref[...]) + bo_ref[...]
    while nxt_dn[0] < n_mlp:
        acc = mlp_fill(acc)
    assert not hs and nxt_up[0] == n_mlp
    z1_scr[...] = ALPHA * x + out
    acc_scr[...] = acc

    k0_scr[0:blk, :] = k0_scr[tt:tt + blk, :]
    k1_scr[0:blk, :] = k1_scr[tt:tt + blk, :]
    v_scr[0:blk, :] = v_scr[tt:tt + blk, :]

    @pl.when(jnp.logical_and(t == tiles_per_seq - 1, s_id < n_tiles))
    def _():
        kout_ref[...] = kl_scr[...]
        vout_ref[...] = vl_scr[...]


def _swa_weight_specs():
    return [_const_spec((D_MODEL, QKV_DIM)), _const_spec((1, QKV_DIM)),
            _const_spec((Q_DIM, D_MODEL)), _const_spec((1, D_MODEL)),
            _const_spec((1, D_MODEL)), _const_spec((1, D_MODEL))]


def _layer1_prompt(x, sinks, swa_weights, mlp_weights):
    bsz, seq, _ = x.shape
    tt = SWA_TT
    assert seq % tt == 0 and seq >= WINDOW and tt % WINDOW == 0
    tps = seq // tt
    n_tiles = bsz * tps

    def swa_tile(s):
        s = jnp.minimum(s, n_tiles - 1)
        return (s // tps, s % tps, 0)

    def mlp_tile(s):
        s = jnp.maximum(s - 2, 0)
        return (s // tps, s % tps, 0)

    kv_spec = pl.BlockSpec((None, WINDOW, KV_DIM), lambda s: (jnp.minimum(s, n_tiles - 1) // tps, 0, 0))
    return pl.pallas_call(
        functools.partial(_layer1_prompt_kernel, tiles_per_seq=tps),
        grid=(n_tiles + 2,),
        in_specs=[pl.BlockSpec(memory_space=pltpu.SMEM), pl.BlockSpec((None, tt, D_MODEL), swa_tile)]
        + _swa_weight_specs() + _mlp_weight_specs(),
        out_specs=[pl.BlockSpec((None, tt, D_MODEL), mlp_tile), kv_spec, kv_spec],
        out_shape=[jax.ShapeDtypeStruct((bsz, seq, D_MODEL), F32),
                   jax.ShapeDtypeStruct((bsz, WINDOW, KV_DIM), F32),
                   jax.ShapeDtypeStruct((bsz, WINDOW, KV_DIM), F32)],
        scratch_shapes=[pltpu.VMEM((tt, Q_DIM), BF16),
                        pltpu.VMEM((tt + WINDOW, KV_DIM), BF16),
                        pltpu.VMEM((tt + WINDOW, KV_DIM), BF16),
                        pltpu.VMEM((tt + WINDOW, KV_DIM), BF16),
                        pltpu.VMEM((tt, Q_DIM), BF16),
                        pltpu.VMEM((WINDOW, KV_DIM), F32),
                        pltpu.VMEM((WINDOW, KV_DIM), F32),
                        pltpu.VMEM((2, tt, D_MODEL), F32),
                        pltpu.VMEM((tt, D_MODEL), F32),
                        pltpu.VMEM((tt, D_MODEL), F32)],
        compiler_params=pltpu.CompilerParams(
            dimension_semantics=("arbitrary",), vmem_limit_bytes=VMEM_LIMIT),
        name="layer1_prompt",
    )(sinks, x, *swa_weights, *mlp_weights)


def _swa_sample_kernel(sink_ref, x_ref, ck_ref, cv_ref, wqkv_ref, bqkv_ref, wo_ref, bo_ref, g_ref, b_ref,
                       y_ref, ks_ref, vs_ref,
                       q_scr, knew_scr, vnew_scr, o_scr, *, nt):
    pt = SMP_PAD_T
    wbuf = ck_ref.shape[1]
    nkeys = 2 * WINDOW
    step = pl.program_id(0)

    @pl.when(step == 0)
    def _():
        qkv = _dot(x_ref[...].astype(BF16), wqkv_ref[...]) + bqkv_ref[...]
        q_scr[...] = qkv[:, :Q_DIM] * (HEAD_DIM ** -0.5)
        knew_scr[...] = qkv[:, Q_DIM:Q_DIM + KV_DIM]
        vnew_scr[...] = qkv[:, Q_DIM + KV_DIM:]

    rows = GROUP * pt
    ri = lax.broadcasted_iota(jnp.int32, (rows, nkeys), 0)
    kj = lax.broadcasted_iota(jnp.int32, (rows, nkeys), 1)
    tq = ri % pt
    in_buf = kj < wbuf
    mask = (in_buf & (kj - wbuf + WINDOW > tq)) | (
        jnp.logical_not(in_buf) & (kj - wbuf <= tq) & (kj - wbuf < nt))
    gi = lax.broadcasted_iota(jnp.int32, (rows, 1), 0) // pt
    sink_cols = []
    for kh in range(N_KV):
        col = jnp.zeros((rows, 1), F32)
        for g in range(GROUP):
            col = jnp.where(gi == g, sink_ref[kh * GROUP + g], col)
        sink_cols.append(col)
    kv0_k = _lane_is_kv0((nkeys, KV_DIM))
    kv0_o = _lane_is_kv0((rows, KV_DIM))

    pad_rows = jnp.zeros((nkeys - wbuf - pt, KV_DIM), F32)
    tail_is_old = lax.broadcasted_iota(jnp.int32, (pt, KV_DIM), 0) < pt - nt

    def per_group(gi, carry):
        bs = [gi * SMP_GROUP + i for i in range(SMP_GROUP)]
        staged = []
        for b in bs:
            r0 = pl.multiple_of((step * SMP_BB + b) * pt, pt)
            k_new = knew_scr[pl.ds(r0, pt), :]
            v_new = vnew_scr[pl.ds(r0, pt), :]
            kb = jnp.concatenate([ck_ref[b], k_new, pad_rows], axis=0)
            vb = jnp.concatenate([cv_ref[b], v_new, pad_rows], axis=0).astype(BF16)
            qs = jnp.concatenate(
                [q_scr[pl.ds(r0, pt), g * KV_DIM:(g + 1) * KV_DIM] for g in range(GROUP)], axis=0)
            qs = qs.astype(BF16)
            scores = [_dot_nt(qs, jnp.where(kv0_k, kb, 0.0).astype(BF16)),
                      _dot_nt(qs, jnp.where(kv0_k, 0.0, kb).astype(BF16))]
            staged.append((b, r0, k_new, v_new, vb, scores))
        soft = [[_sink_softmax(sc[kh], mask, sink_cols[kh]) for kh in range(N_KV)]
                for (_, _, _, _, _, sc) in staged]
        for (b, r0, k_new, v_new, vb, _), pr in zip(staged, soft):
            o0 = _dot(pr[0][0], vb) * pr[0][1]
            o1 = _dot(pr[1][0], vb) * pr[1][1]
            o = jnp.where(kv0_o, o0, o1)
            for g in range(GROUP):
                o_scr[pl.ds(r0, pt), g * KV_DIM:(g + 1) * KV_DIM] = o[g * pt:(g + 1) * pt, :]
            for new, c_ref, out_ref in ((k_new, ck_ref, ks_ref), (v_new, cv_ref, vs_ref)):
                out_ref[b, 0:wbuf - pt, :] = c_ref[b, nt:wbuf - pt + nt, :]
                old_tail = pltpu.roll(c_ref[b, wbuf - pt:wbuf, :], pt - nt, axis=0)
                out_ref[b, wbuf - pt:wbuf, :] = jnp.where(
                    tail_is_old, old_tail, pltpu.roll(new, pt - nt, axis=0))
        return carry

    lax.fori_loop(0, SMP_BB // SMP_GROUP, per_group, 0)

    @pl.when(step == pl.num_programs(0) - 1)
    def _():
        out = _dot(o_scr[...].astype(BF16), wo_ref[...]) + bo_ref[...]
        y_ref[...] = _layer_norm(ALPHA * x_ref[...] + out, g_ref[...], b_ref[...])


def _swa_sample(x_pad, cache_k, cache_v, sinks, weights, nt):
    rows = x_pad.shape[0]
    nb, wbuf, _ = cache_k.shape
    assert nb % SMP_BB == 0 and rows == nb * SMP_PAD_T and nt <= SMP_PAD_T
    assert wbuf + SMP_PAD_T <= 2 * WINDOW and wbuf % SMP_PAD_T == 0
    cache_spec = pl.BlockSpec((SMP_BB, wbuf, KV_DIM), lambda i: (i, 0, 0))
    return pl.pallas_call(
        functools.partial(_swa_sample_kernel, nt=nt),
        grid=(nb // SMP_BB,),
        in_specs=[pl.BlockSpec(memory_space=pltpu.SMEM), _const_spec((rows, D_MODEL)),
                  cache_spec, cache_spec] + _swa_weight_specs(),
        out_specs=[_const_spec((rows, D_MODEL)), cache_spec, cache_spec],
        out_shape=[jax.ShapeDtypeStruct((rows, D_MODEL), F32),
                   jax.ShapeDtypeStruct((nb, wbuf, KV_DIM), F32),
                   jax.ShapeDtypeStruct((nb, wbuf, KV_DIM), F32)],
        scratch_shapes=[pltpu.VMEM((rows, Q_DIM), F32),
                        pltpu.VMEM((rows, KV_DIM), F32),
                        pltpu.VMEM((rows, KV_DIM), F32),
                        pltpu.VMEM((rows, Q_DIM), F32)],
        compiler_params=pltpu.CompilerParams(
            dimension_semantics=("arbitrary",), vmem_limit_bytes=VMEM_LIMIT),
        name="swa_sample",
    )(sinks, x_pad, cache_k, cache_v, *weights)


def _row(v):
    return v.reshape(1, -1).astype(F32)


def _pack_lru_weights(j, lru_w_x, lru_b_x, lru_w_y, lru_b_y, lru_conv_w, lru_conv_b,
                      lru_w_ga, lru_b_ga, lru_w_gi, lru_b_gi, lru_lam, lru_w_out, lru_b_out, g, b):
    def chunk_pack(ax, ay):
        lead = ax.shape[:-1]
        cw2 = 2 * LRU_BLOCK
        return jnp.concatenate([ax.reshape(*lead, -1, cw2), ay.reshape(*lead, -1, cw2)],
                               axis=-1).reshape(*lead, 2 * D_RNN)

    wxy = chunk_pack(lru_w_x[j].astype(BF16), lru_w_y[j].astype(BF16))
    bxy = chunk_pack(lru_b_x[j], lru_b_y[j]).reshape(1, -1)
    wg = jnp.concatenate([lru_w_ga[j], lru_w_gi[j]], axis=2).astype(BF16)
    bg = jnp.concatenate([lru_b_ga[j], lru_b_gi[j]], axis=1)[:, None, :]
    return (wxy, bxy, lru_conv_w[j], _row(lru_conv_b[j]), wg, bg, _row(lru_lam[j]),
            lru_w_out[j].astype(BF16), _row(lru_b_out[j]), _row(g), _row(b))


def _pack_swa_weights(j, swa_w_qkv, swa_b_qkv, swa_w_o, swa_b_o, g, b):
    w, bias = swa_w_qkv[j].astype(BF16), swa_b_qkv[j]
    wq = w[:, :Q_DIM].reshape(D_MODEL, N_KV, GROUP, HEAD_DIM).transpose(0, 2, 1, 3).reshape(D_MODEL, Q_DIM)
    bq = bias[:Q_DIM].reshape(N_KV, GROUP, HEAD_DIM).transpose(1, 0, 2).reshape(Q_DIM)
    wqkv = jnp.concatenate([wq, w[:, Q_DIM:]], axis=1)
    bqkv = jnp.concatenate([bq, bias[Q_DIM:]]).reshape(1, -1)
    wo = swa_w_o[j].astype(BF16).reshape(N_KV, GROUP, HEAD_DIM, D_MODEL).transpose(1, 0, 2, 3)
    return (wqkv, bqkv, wo.reshape(Q_DIM, D_MODEL), _row(swa_b_o[j]), _row(g), _row(b))


def kernel(x_prompt, x_sample, state_lru_h, state_lru_conv, cache_swa_k, cache_swa_v,
           lru_w_x, lru_b_x, lru_w_y, lru_b_y, lru_conv_w, lru_conv_b,
           lru_w_ga, lru_b_ga, lru_w_gi, lru_b_gi, lru_lam, lru_w_out, lru_b_out,
           swa_w_qkv, swa_b_qkv, swa_sinks, swa_w_o, swa_b_o,
           mlp_w_up, mlp_w_down, ln1_g, ln1_b, ln2_g, ln2_b):
    bsz, seq, _ = x_prompt.shape
    nb, nt, _ = x_sample.shape
    wbuf = cache_swa_k.shape[2]

    mlp_w = [(mlp_w_up[i].astype(BF16), mlp_w_down[i].astype(BF16), _row(ln2_g[i]), _row(ln2_b[i]))
             for i in range(DEPTH)]

    def mlp(x2d, i):
        return _mlp_block(x2d, *mlp_w[i])

    lw = _pack_lru_weights(0, lru_w_x, lru_b_x, lru_w_y, lru_b_y, lru_conv_w, lru_conv_b,
                           lru_w_ga, lru_b_ga, lru_w_gi, lru_b_gi, lru_lam, lru_w_out, lru_b_out,
                           ln1_g[0], ln1_b[0])
    wg = lw[4]
    zero = jnp.zeros_like(wg[0::2])
    wg_pairs = jnp.concatenate([jnp.concatenate([wg[0::2], zero], axis=2),
                                jnp.concatenate([zero, wg[1::2]], axis=2)], axis=1)
    lw_p = lw[:4] + (wg_pairs,) + lw[5:]
    xp, h_p, conv_p = _layer0_prompt(x_prompt, lw_p, mlp_w[0])

    xs_tm = x_sample.transpose(1, 0, 2).reshape(nt * nb, D_MODEL)
    cs_tm = state_lru_conv[0].transpose(1, 0, 2).reshape((CONV_W - 1) * nb, D_RNN)
    xs_tm, h_s, conv_s_tm = _lru_sample(xs_tm, state_lru_h[0], cs_tm, lw, nb, nt)
    xs_tm = mlp(xs_tm, 0)
    conv_s = conv_s_tm.reshape(CONV_W - 1, nb, D_RNN).transpose(1, 0, 2)

    sw = _pack_swa_weights(0, swa_w_qkv, swa_b_qkv, swa_w_o, swa_b_o, ln1_g[1], ln1_b[1])
    sinks = swa_sinks[0].astype(F32)
    xp, k_p, v_p = _layer1_prompt(xp, sinks, sw, mlp_w[1])

    xs_bm = xs_tm.reshape(nt, nb, D_MODEL).transpose(1, 0, 2)
    xs_pad = jnp.pad(xs_bm, ((0, 0), (0, SMP_PAD_T - nt), (0, 0))).reshape(nb * SMP_PAD_T, D_MODEL)
    ck = cache_swa_k[0].reshape(nb, wbuf, KV_DIM)
    cv = cache_swa_v[0].reshape(nb, wbuf, KV_DIM)
    ys_pad, k_s, v_s = _swa_sample(xs_pad, ck, cv, sinks, sw, nt)
    xs = ys_pad.reshape(nb, SMP_PAD_T, D_MODEL)[:, :nt].reshape(nb * nt, D_MODEL)
    xs = mlp(xs, 1).reshape(nb, nt, D_MODEL)

    kv_shape = (1, -1, min(WINDOW, seq), N_KV, HEAD_DIM)
    return (xp, xs,
            h_p.reshape(1, bsz, D_RNN), conv_p[None],
            k_p.reshape(kv_shape), v_p.reshape(kv_shape),
            h_s[None], conv_s[None],
            k_s.reshape(1, nb, wbuf, N_KV, HEAD_DIM), v_s.reshape(1, nb, wbuf, N_KV, HEAD_DIM))
```

```python
import functools

import jax
import jax.numpy as jnp
from jax import lax
from jax.experimental import pallas as pl
from jax.experimental.pallas import tpu as pltpu

F32 = jnp.float32
BF16 = jnp.bfloat16

D_MODEL = 1024
D_RNN = 1024
D_FF = 4096
LRU_BLOCKS = 8
LRU_BLOCK = 128
CONV_W = 4
LRU_C = 8.0
N_HEADS = 16
HEAD_DIM = 64
N_KV = 2
GROUP = 8
WINDOW = 128
KV_DIM = N_KV * HEAD_DIM
Q_DIM = N_HEADS * HEAD_DIM
QKV_DIM = Q_DIM + 2 * KV_DIM
DEPTH = 2
ALPHA = (2.0 * DEPTH) ** 0.25
LN_EPS = 1e-5

SUBLANES = 8
VMEM_LIMIT = 60 * 1024 * 1024

MLP_TM = 512
MLP_FC = 1024
LRU_TT = 512
LRU_IL = 256
SWA_TT = 512
SWA_PAIR = 2
L1_MLP_CHUNKS = 8
SMP_PAD_T = 8
SMP_BB = 16
SMP_GROUP = 4


def _const_spec(shape):
    nd = len(shape)
    return pl.BlockSpec(shape, lambda *_: (0,) * nd, pipeline_mode=pl.Buffered(1))


def _dot(a, b):
    return jnp.dot(a, b, preferred_element_type=F32)


def _dot_nt(a, b):
    return lax.dot_general(a, b, (((1,), (1,)), ((), ())), preferred_element_type=F32)


def _layer_norm(z, g, b):
    mu = jnp.mean(z, axis=-1, keepdims=True)
    d = z - mu
    var = jnp.mean(d * d, axis=-1, keepdims=True)
    return d * lax.rsqrt(var + LN_EPS) * g + b


def _sigmoid(x):
    return 0.5 * jnp.tanh(0.5 * x) + 0.5


def _mlp_acc(x, wup_ref, wdn_ref):
    xb = x.astype(BF16)
    acc = jnp.zeros(x.shape, F32)
    for j in range(D_FF // MLP_FC):
        h = _dot(xb, wup_ref[:, j * MLP_FC:(j + 1) * MLP_FC])
        h = jnp.square(jnp.maximum(h, 0.0)).astype(BF16)
        acc = acc + _dot(h, wdn_ref[j * MLP_FC:(j + 1) * MLP_FC, :])
    return acc


def _mlp_ln(x, wup_ref, wdn_ref, g_ref, b_ref):
    return _layer_norm(ALPHA * x + _mlp_acc(x, wup_ref, wdn_ref), g_ref[...], b_ref[...])


def _mlp_kernel(x_ref, wup_ref, wdn_ref, g_ref, b_ref, o_ref):
    o_ref[...] = _mlp_ln(x_ref[...], wup_ref, wdn_ref, g_ref, b_ref)


def _mlp_weight_specs():
    return [_const_spec((D_MODEL, D_FF)), _const_spec((D_FF, D_MODEL)),
            _const_spec((1, D_MODEL)), _const_spec((1, D_MODEL))]


def _mlp_block(x, w_up, w_dn, g, b):
    n = x.shape[0]
    tm = min(MLP_TM, n)
    assert n % tm == 0
    row_spec = pl.BlockSpec((tm, D_MODEL), lambda i: (i, 0))
    return pl.pallas_call(
        _mlp_kernel,
        grid=(n // tm,),
        in_specs=[row_spec] + _mlp_weight_specs(),
        out_specs=row_spec,
        out_shape=jax.ShapeDtypeStruct((n, D_MODEL), F32),
        compiler_params=pltpu.CompilerParams(
            dimension_semantics=("arbitrary",), vmem_limit_bytes=VMEM_LIMIT),
        name="mlp_ln",
    )(x, w_up, w_dn, g, b)


_GELU_C0 = (2.0 / jnp.pi) ** 0.5
_GELU_C1 = _GELU_C0 * 0.044715


def _gelu_tanh(x):
    t = jnp.tanh(x * (_GELU_C0 + _GELU_C1 * (x * x)))
    hx = 0.5 * x
    return hx + hx * t


def _lru_in_proj(x, wxy_ref, bxy_ref):
    uy = _dot(x.astype(BF16), wxy_ref[...]) + bxy_ref[...]
    cw2 = 2 * LRU_BLOCK
    n_col = D_RNN // cw2
    u = jnp.concatenate([uy[:, 2 * j * cw2:(2 * j + 1) * cw2] for j in range(n_col)], axis=1)
    y_pre = jnp.concatenate([uy[:, (2 * j + 1) * cw2:(2 * j + 2) * cw2] for j in range(n_col)], axis=1)
    return u, _gelu_tanh(y_pre)


def _neg_c_softplus_neg(lam):
    z = -lam
    sp = jnp.maximum(z, 0.0) + jnp.log1p(jnp.exp(-jnp.abs(z)))
    return -LRU_C * sp


def _lru_gate_chunk(xc_n, n, wg_ref, bg_ref, nla):
    return _lru_gate_math(_dot(xc_n.astype(BF16), wg_ref[n]), xc_n, n, bg_ref, nla)


def _lru_gate_math(gt, xc_n, n, bg_ref, nla):
    lo = n * LRU_BLOCK
    gt = gt + bg_ref[n]
    r = _sigmoid(gt[:, :LRU_BLOCK])
    gi = _sigmoid(gt[:, LRU_BLOCK:])
    log_a = r * nla[:, lo:lo + LRU_BLOCK]
    a = jnp.exp(log_a)
    v = -jnp.tanh(log_a) * (a * a + 1.0)
    mult = jnp.where(v > 0.0, v * lax.rsqrt(v), 0.0)
    return a, mult * (gi * xc_n)


def _lru_out(h, y, x, wo_ref, bo_ref, g_ref, b_ref):
    out = _dot((h * y).astype(BF16), wo_ref[...]) + bo_ref[...]
    return _layer_norm(ALPHA * x + out, g_ref[...], b_ref[...])


def _interleave_matrix(tt):
    r = jnp.arange(tt)
    src_time = (r % SUBLANES) * (tt // SUBLANES) + r // SUBLANES
    return (src_time[:, None] == jnp.arange(tt)[None, :]).astype(BF16)


def _groups(z):
    return [z[r:r + LRU_IL, :] for r in range(0, z.shape[0], LRU_IL)]


def _shift_one_step(z, carry_row):
    row = lax.broadcasted_iota(jnp.int32, (SUBLANES, z.shape[1]), 0)
    out = []
    for zg in _groups(z):
        last = zg[LRU_IL - SUBLANES:, :]
        out.append(jnp.where(row == 0, carry_row, pltpu.roll(last, 1, axis=0)))
        out.append(zg[:LRU_IL - SUBLANES, :])
        carry_row = zg[LRU_IL - 1:, :]
    return jnp.concatenate(out, axis=0)


def _scan_interleaved(a, b, h_in):
    parts = []
    for ag, bg in zip(_groups(a), _groups(b)):
        hg, h_in = _scan_group(ag, bg, h_in)
        parts.append(hg)
    return jnp.concatenate(parts, axis=0), h_in


def _scan_group(a, b, h_in):
    n_steps = a.shape[0] // SUBLANES
    h = b[0:SUBLANES, :]
    p = a[0:SUBLANES, :]
    hs, ps = [h], [p]
    for i in range(1, n_steps):
        ai = a[i * SUBLANES:(i + 1) * SUBLANES, :]
        h = ai * h + b[i * SUBLANES:(i + 1) * SUBLANES, :]
        p = ai * p
        hs.append(h)
        ps.append(p)
    row = lax.broadcasted_iota(jnp.int32, h.shape, 0)
    cin = h_in
    cvec = jnp.zeros(h.shape, F32)
    for c in range(SUBLANES):
        cvec = jnp.where(row == c, cin, cvec)
        cin = h[c:c + 1, :] + p[c:c + 1, :] * cin
    full = jnp.concatenate([hs[i] + ps[i] * cvec for i in range(n_steps)], axis=0)
    return full, cin


def _drain_steps(s, n_tiles, g1_ref, b1_ref, wup_ref, wdn_ref, g2_ref, b2_ref, y_ref, x1_scr, z1_scr, acc_scr):
    slot = s % 2

    @pl.when(s == n_tiles)
    def _():
        x1 = _layer_norm(z1_scr[...], g1_ref[...], b1_ref[...])
        x1_scr[slot] = x1
        y_ref[...] = _layer_norm(ALPHA * x1_scr[1 - slot] + acc_scr[...], g2_ref[...], b2_ref[...])
        acc_scr[...] = _mlp_acc(x1, wup_ref, wdn_ref)

    @pl.when(s == n_tiles + 1)
    def _():
        y_ref[...] = _layer_norm(ALPHA * x1_scr[1 - slot] + acc_scr[...], g2_ref[...], b2_ref[...])


def _layer0_prompt_kernel(*refs, tiles_per_seq):
    g1_ref, b1_ref, wup_ref, wdn_ref, g2_ref, b2_ref, y_ref = refs[12:19]
    x1_scr, z1_scr, acc_scr = refs[-3:]
    s = pl.program_id(0)
    n_tiles = pl.num_programs(0) - 2

    @pl.when(s < n_tiles)
    def _():
        _layer0_tile_step(*refs, tiles_per_seq=tiles_per_seq)

    _drain_steps(s, n_tiles, g1_ref, b1_ref, wup_ref, wdn_ref, g2_ref, b2_ref, y_ref, x1_scr, z1_scr, acc_scr)


def _layer0_tile_step(x_ref, pm_ref, pmt_ref, wxy_ref, bxy_ref, cw_ref, cb_ref, wg_ref, bg_ref, lam_ref,
                      wo_ref, bo_ref, g1_ref, b1_ref, wup_ref, wdn_ref, g2_ref, b2_ref,
                      y_ref, hout_ref, cout_ref,
                      hy_scr, hc_scr, cc_scr, ul_scr, x1_scr, z1_scr, acc_scr, *, tiles_per_seq):
    tt = LRU_TT
    s = pl.program_id(0)
    n_tiles = pl.num_programs(0) - 2
    t = s % tiles_per_seq
    slot = s % 2

    @pl.when(s == 0)
    def _():
        x1_scr[1] = jnp.zeros((tt, D_MODEL), F32)
        z1_scr[...] = jnp.zeros((tt, D_MODEL), F32)
        acc_scr[...] = jnp.zeros((tt, D_MODEL), F32)

    @pl.when(t == 0)
    def _():
        hc_scr[...] = jnp.zeros(hc_scr.shape, F32)
        cc_scr[...] = jnp.zeros(cc_scr.shape, F32)

    x = x_ref[...]
    xb = x.astype(BF16)
    acc = jnp.zeros((tt, D_MODEL), F32)
    nla = _neg_c_softplus_neg(lam_ref[...])
    n_mlp = LRU_BLOCKS
    fc = D_FF // n_mlp
    cw2 = 2 * LRU_BLOCK
    n_col = D_RNN // cw2

    def mlp_up(n):
        h = _dot(x1b, wup_ref[:, n * fc:(n + 1) * fc])
        return jnp.square(jnp.maximum(h, 0.0)).astype(BF16)

    def mlp_down(acc, h, n):
        return acc + _dot(h, wdn_ref[n * fc:(n + 1) * fc, :])

    def in_proj(j):
        return _dot(xpb, wxy_ref[:, 2 * j * cw2:2 * (j + 1) * cw2]) + bxy_ref[:, 2 * j * cw2:2 * (j + 1) * cw2]

    xp = jnp.concatenate([_dot(pm_ref[...], g) for g in _groups(xb)], axis=0)
    xpb = xp.astype(BF16)
    uy = in_proj(0)
    x1 = _layer_norm(z1_scr[...], g1_ref[...], b1_ref[...])
    x1_scr[slot] = x1
    x1b = x1.astype(BF16)
    hs = {}

    def mlp_ops(acc, ops):
        for kind, n in ops:
            if kind == "u":
                hs[n] = mlp_up(n)
            else:
                acc = mlp_down(acc, hs.pop(n), n)
        return acc

    phase_ops = [[("u", 1), ("d", 0), ("u", 2)], [("d", 1), ("u", 3), ("d", 2)],
                 [("u", 4), ("d", 3), ("u", 5)], [("d", 4), ("u", 6), ("d", 5)]]
    acc = mlp_ops(acc, [("u", 0)])
    for j in range(n_col):
        c0 = j * cw2
        acc = mlp_ops(acc, phase_ops[j])
        if j == 0:
            y_ref[...] = _layer_norm(ALPHA * x1_scr[1 - slot] + acc_scr[...], g2_ref[...], b2_ref[...])
        u = uy[:, :cw2]
        y = _gelu_tanh(uy[:, cw2:])
        z = cw_ref[0:1, c0:c0 + cw2] * u
        for k in range(1, CONV_W):
            carry = cc_scr[k - 1:k, c0:c0 + cw2]
            cc_scr[k - 1:k, c0:c0 + cw2] = z[tt - 1:tt, :]
            z = cw_ref[k:k + 1, c0:c0 + cw2] * u + _shift_one_step(z, carry)
        xc = z + cb_ref[:, c0:c0 + cw2]
        ul_scr[:, c0:c0 + cw2] = u[tt - (CONV_W - 1) * SUBLANES:, :]
        xcs = [xc[:, q * LRU_BLOCK:(q + 1) * LRU_BLOCK] for q in range(2)]
        gt2 = _dot(xc.astype(BF16), wg_ref[j])
        gts = [gt2[:, q * cw2:(q + 1) * cw2] for q in range(2)]
        if j + 1 < n_col:
            uy = in_proj(j + 1)
        for q in range(2):
            n = 2 * j + q
            lo = n * LRU_BLOCK
            a, b = _lru_gate_math(gts[q], xcs[q], n, bg_ref, nla)
            h, h_end = _scan_interleaved(a, b, hc_scr[0:1, lo:lo + LRU_BLOCK])
            hc_scr[0:1, lo:lo + LRU_BLOCK] = h_end
            hy_scr[:, lo:lo + LRU_BLOCK] = (h * y[:, q * LRU_BLOCK:(q + 1) * LRU_BLOCK]).astype(BF16)

    acc = mlp_ops(acc, [("u", 7), ("d", 6)])
    hy = jnp.concatenate([_dot(pmt_ref[...], g) for g in _groups(hy_scr[...])], axis=0)
    acc = mlp_ops(acc, [("d", 7)])
    out = _dot(hy.astype(BF16), wo_ref[...]) + bo_ref[...]
    assert not hs and n_mlp == 8 and n_col == len(phase_ops)
    z1_scr[...] = ALPHA * x + out
    acc_scr[...] = acc

    @pl.when(jnp.logical_and(t == tiles_per_seq - 1, s < n_tiles))
    def _():
        hout_ref[...] = hc_scr[0:1, :]
        for k in range(CONV_W - 1):
            r = (k + 1) * SUBLANES - 1
            cout_ref[k:k + 1, :] = ul_scr[r:r + 1, :]


def _lru_sample_kernel(x_ref, h0_ref, cs_ref, wxy_ref, bxy_ref, cw_ref, cb_ref, wg_ref, bg_ref,
                       lam_ref, wo_ref, bo_ref, g_ref, b_ref,
                       y_ref, hout_ref, cout_ref, *, nb, nt):
    x = x_ref[...]
    u, y = _lru_in_proj(x, wxy_ref, bxy_ref)
    upad = jnp.concatenate([cs_ref[...], u], axis=0)
    xc = cb_ref[...]
    for k in range(CONV_W):
        xc = xc + cw_ref[k:k + 1, :] * upad[k * nb:(k + nt) * nb, :]
    nla = _neg_c_softplus_neg(lam_ref[...])
    a_parts, b_parts = [], []
    for n in range(LRU_BLOCKS):
        lo = n * LRU_BLOCK
        a, b = _lru_gate_chunk(xc[:, lo:lo + LRU_BLOCK], n, wg_ref, bg_ref, nla)
        a_parts.append(a)
        b_parts.append(b)
    a = jnp.concatenate(a_parts, axis=1)
    b = jnp.concatenate(b_parts, axis=1)
    h = h0_ref[...]
    hs = []
    for t in range(nt):
        h = a[t * nb:(t + 1) * nb, :] * h + b[t * nb:(t + 1) * nb, :]
        hs.append(h)
    y_ref[...] = _lru_out(jnp.concatenate(hs, axis=0), y, x, wo_ref, bo_ref, g_ref, b_ref)
    hout_ref[...] = h
    cout_ref[...] = upad[nt * nb:, :]


def _lru_weight_specs(paired_gates=False):
    pair = 2 if paired_gates else 1
    return [_const_spec((D_MODEL, 2 * D_RNN)), _const_spec((1, 2 * D_RNN)),
            _const_spec((CONV_W, D_RNN)), _const_spec((1, D_RNN)),
            _const_spec((LRU_BLOCKS // pair, pair * LRU_BLOCK, pair * 2 * LRU_BLOCK)),
            _const_spec((LRU_BLOCKS, 1, 2 * LRU_BLOCK)),
            _const_spec((1, D_RNN)), _const_spec((D_RNN, D_MODEL)), _const_spec((1, D_MODEL)),
            _const_spec((1, D_MODEL)), _const_spec((1, D_MODEL))]


def _layer0_prompt(x, lru_weights, mlp_weights):
    bsz, seq, _ = x.shape
    tt = LRU_TT
    assert seq % tt == 0
    tps = seq // tt
    n_tiles = bsz * tps

    def lru_tile(s):
        s = jnp.minimum(s, n_tiles - 1)
        return (s // tps, s % tps, 0)

    def mlp_tile(s):
        s = jnp.maximum(s - 2, 0)
        return (s // tps, s % tps, 0)

    def state_block(s):
        return (jnp.minimum(s, n_tiles - 1) // tps, 0, 0)

    assert tt % LRU_IL == 0
    pm = _interleave_matrix(LRU_IL)
    return pl.pallas_call(
        functools.partial(_layer0_prompt_kernel, tiles_per_seq=tps),
        grid=(n_tiles + 2,),
        in_specs=[pl.BlockSpec((None, tt, D_MODEL), lru_tile), _const_spec((LRU_IL, LRU_IL)),
                  _const_spec((LRU_IL, LRU_IL))]
        + _lru_weight_specs(paired_gates=True) + _mlp_weight_specs(),
        out_specs=[pl.BlockSpec((None, tt, D_MODEL), mlp_tile),
                   pl.BlockSpec((None, 1, D_RNN), state_block),
                   pl.BlockSpec((None, CONV_W - 1, D_RNN), state_block)],
        out_shape=[jax.ShapeDtypeStruct((bsz, seq, D_MODEL), F32),
                   jax.ShapeDtypeStruct((bsz, 1, D_RNN), F32),
                   jax.ShapeDtypeStruct((bsz, CONV_W - 1, D_RNN), F32)],
        scratch_shapes=[pltpu.VMEM((tt, D_RNN), BF16),
                        pltpu.VMEM((SUBLANES, D_RNN), F32),
                        pltpu.VMEM((SUBLANES, D_RNN), F32),
                        pltpu.VMEM(((CONV_W - 1) * SUBLANES, D_RNN), F32),
                        pltpu.VMEM((2, tt, D_MODEL), F32),
                        pltpu.VMEM((tt, D_MODEL), F32),
                        pltpu.VMEM((tt, D_MODEL), F32)],
        compiler_params=pltpu.CompilerParams(
            dimension_semantics=("arbitrary",), vmem_limit_bytes=VMEM_LIMIT),
        name="layer0_prompt",
    )(x, pm, pm.T, *lru_weights, *mlp_weights)


def _lru_sample(x_tm, h0, cs_tm, weights, nb, nt):
    rows = nb * nt
    return pl.pallas_call(
        functools.partial(_lru_sample_kernel, nb=nb, nt=nt),
        grid=(1,),
        in_specs=[_const_spec((rows, D_MODEL)), _const_spec((nb, D_RNN)),
                  _const_spec(((CONV_W - 1) * nb, D_RNN))] + _lru_weight_specs(),
        out_specs=[_const_spec((rows, D_MODEL)), _const_spec((nb, D_RNN)),
                   _const_spec(((CONV_W - 1) * nb, D_RNN))],
        out_shape=[jax.ShapeDtypeStruct((rows, D_MODEL), F32),
                   jax.ShapeDtypeStruct((nb, D_RNN), F32),
                   jax.ShapeDtypeStruct(((CONV_W - 1) * nb, D_RNN), F32)],
        compiler_params=pltpu.CompilerParams(
            dimension_semantics=("arbitrary",), vmem_limit_bytes=VMEM_LIMIT),
        name="lru_sample",
    )(x_tm, h0, cs_tm, *weights)


def _lane_is_kv0(shape):
    return lax.broadcasted_iota(jnp.int32, shape, len(shape) - 1) < HEAD_DIM


def _sink_softmax(s, mask, sink):
    s = jnp.where(mask, s, -jnp.inf)
    m = jnp.maximum(jnp.max(s, axis=-1, keepdims=True), sink)
    p = jnp.exp(s - m)
    denom = jnp.sum(p, axis=-1, keepdims=True) + jnp.exp(sink - m)
    return p.astype(BF16), 1.0 / denom


def _sink_softmax_t(s, mask, sink):
    s = jnp.where(mask, s, -jnp.inf)
    m = jnp.maximum(jnp.max(s, axis=0, keepdims=True), sink)
    p = jnp.exp(s - m)
    denom = jnp.sum(p, axis=0, keepdims=True) + jnp.exp(sink - m)
    return p.astype(BF16), 1.0 / denom


def _layer1_prompt_kernel(*refs, tiles_per_seq):
    g1_ref, b1_ref, wup_ref, wdn_ref, g2_ref, b2_ref, y_ref = refs[6:13]
    x1_scr, z1_scr, acc_scr = refs[-3:]
    s = pl.program_id(0)
    n_tiles = pl.num_programs(0) - 2

    @pl.when(s < n_tiles)
    def _():
        _layer1_tile_step(*refs, tiles_per_seq=tiles_per_seq)

    _drain_steps(s, n_tiles, g1_ref, b1_ref, wup_ref, wdn_ref, g2_ref, b2_ref, y_ref, x1_scr, z1_scr, acc_scr)


def _layer1_tile_step(sink_ref, x_ref, wqkv_ref, bqkv_ref, wo_ref, bo_ref, g1_ref, b1_ref,
                      wup_ref, wdn_ref, g2_ref, b2_ref,
                      y_ref, kout_ref, vout_ref,
                      q_scr, k0_scr, k1_scr, vt_scr, o_scr, kl_scr, vl_scr, x1_scr, z1_scr, acc_scr,
                      *, tiles_per_seq):
    tt = SWA_TT
    blk = WINDOW
    s_id = pl.program_id(0)
    n_tiles = pl.num_programs(0) - 2
    t = s_id % tiles_per_seq
    slot = s_id % 2

    @pl.when(s_id == 0)
    def _():
        x1_scr[1] = jnp.zeros((tt, D_MODEL), F32)
        z1_scr[...] = jnp.zeros((tt, D_MODEL), F32)
        acc_scr[...] = jnp.zeros((tt, D_MODEL), F32)

    @pl.when(t == 0)
    def _():
        zeros = jnp.zeros((blk, KV_DIM), BF16)
        k0_scr[0:blk, :] = zeros
        k1_scr[0:blk, :] = zeros
        vt_scr[:, 0:blk] = jnp.zeros((KV_DIM, blk), BF16)

    x = x_ref[...]
    xb = x.astype(BF16)
    acc = jnp.zeros((tt, D_MODEL), F32)
    n_mlp = L1_MLP_CHUNKS
    fc = D_FF // n_mlp
    hs = {}
    nxt_up = [0]
    nxt_dn = [0]

    def mlp_up():
        n = nxt_up[0]
        nxt_up[0] += 1
        h = _dot(x1b, wup_ref[:, n * fc:(n + 1) * fc])
        hs[n] = jnp.square(jnp.maximum(h, 0.0)).astype(BF16)

    def mlp_down(acc):
        n = nxt_dn[0]
        nxt_dn[0] += 1
        return acc + _dot(hs.pop(n), wdn_ref[n * fc:(n + 1) * fc, :])

    def mlp_fill(acc):
        if nxt_up[0] < n_mlp:
            mlp_up()
        if nxt_dn[0] < nxt_up[0] - 1 or (nxt_up[0] == n_mlp and nxt_dn[0] < n_mlp):
            acc = mlp_down(acc)
        return acc

    qkv =_dot(xb, wqkv_ref[...]) + bqkv_ref[...]
    x1 = _layer_norm(z1_scr[...], g1_ref[...], b1_ref[...])
    x1_scr[slot] = x1
    x1b = x1.astype(BF16)
    mlp_up()
    if n_mlp - 2 * (tt // blk) >= 4:
        acc = mlp_fill(acc)
    y_ref[...] = _layer_norm(ALPHA * x1_scr[1 - slot] + acc_scr[...], g2_ref[...], b2_ref[...])
    k = qkv[:, Q_DIM:Q_DIM + KV_DIM]
    v = qkv[:, Q_DIM + KV_DIM:]
    q_scr[...] = (qkv[:, :Q_DIM] * (HEAD_DIM ** -0.5)).astype(BF16)
    kv0 = _lane_is_kv0((tt, KV_DIM))
    k0_scr[blk:, :] = jnp.where(kv0, k, 0.0).astype(BF16)
    k1_scr[blk:, :] = jnp.where(kv0, 0.0, k).astype(BF16)
    for i in range(tt // blk):
        vt_scr[:, (i + 1) * blk:(i + 2) * blk] = v[i * blk:(i + 1) * blk, :].T.astype(BF16)
    kl_scr[...] = k[tt - blk:, :]
    vl_scr[...] = v[tt - blk:, :]

    kj = lax.broadcasted_iota(jnp.int32, (2 * blk, blk), 0)
    qi = lax.broadcasted_iota(jnp.int32, (2 * blk, blk), 1)
    band = (kj > qi) & (kj <= qi + blk)
    kv0_rows = lax.broadcasted_iota(jnp.int32, (KV_DIM, GROUP * blk), 0) < HEAD_DIM

    n_blk = tt // blk
    def score_matmul(i):
        r0 = i * blk
        qs = jnp.concatenate(
            [q_scr[r0:r0 + blk, g * KV_DIM:(g + 1) * KV_DIM] for g in range(GROUP)], axis=0)
        kband = jnp.concatenate([k0_scr[r0:r0 + 2 * blk, :], k1_scr[r0:r0 + 2 * blk, :]], axis=0)
        return _dot_nt(kband, qs)

    def softmax_pv(i, scores_i):
        first = jnp.logical_and(t == 0, i == 0)
        mask = band & (kj >= jnp.where(first, blk, 0))
        ps, rs = [], []
        for kh in range(N_KV):
            s_kh = scores_i[kh * 2 * blk:(kh + 1) * 2 * blk, :]
            for g in range(GROUP):
                p, r = _sink_softmax_t(s_kh[:, g * blk:(g + 1) * blk], mask, sink_ref[kh * GROUP + g])
                ps.append(p)
                rs.append(r)
        ot = _dot(vt_scr[:, i * blk:(i + 2) * blk], jnp.concatenate(ps, axis=1))
        return ot, jnp.concatenate(rs, axis=1)

    def store_out(i, ot, r_all):
        r0 = i * blk
        ot = ot * r_all
        half = GROUP * blk
        ot = jnp.where(kv0_rows, ot[:, :half], ot[:, half:])
        for g in range(GROUP):
            o_scr[r0:r0 + blk, g * KV_DIM:(g + 1) * KV_DIM] = ot[:, g * blk:(g + 1) * blk].T.astype(BF16)

    for i0 in range(0, n_blk, SWA_PAIR):
        pair = range(i0, min(i0 + SWA_PAIR, n_blk))
        scores = {}
        for i in pair:
            scores[i] = score_matmul(i)
            acc = mlp_fill(acc)
        for i in pair:
            o, rs = softmax_pv(i, scores[i])
            acc = mlp_fill(acc)
            store_out(i, o, rs)

    out =_dot(o_scr[...], wo_ref[...]) + bo_ref[...]
    while nxt_dn[0] < n_mlp:
        acc = mlp_fill(acc)
    assert not hs and nxt_up[0] == n_mlp
    z1_scr[...] = ALPHA * x + out
    acc_scr[...] = acc

    k0_scr[0:blk, :] = k0_scr[tt:tt + blk, :]
    k1_scr[0:blk, :] = k1_scr[tt:tt + blk, :]
    vt_scr[:, 0:blk] = vt_scr[:, tt:tt + blk]

    @pl.when(jnp.logical_and(t == tiles_per_seq - 1, s_id < n_tiles))
    def _():
        kout_ref[...] = kl_scr[...]
        vout_ref[...] = vl_scr[...]


def _swa_weight_specs():
    return [_const_spec((D_MODEL, QKV_DIM)), _const_spec((1, QKV_DIM)),
            _const_spec((Q_DIM, D_MODEL)), _const_spec((1, D_MODEL)),
            _const_spec((1, D_MODEL)), _const_spec((1, D_MODEL))]


def _layer1_prompt(x, sinks, swa_weights, mlp_weights):
    bsz, seq, _ = x.shape
    tt = SWA_TT
    assert seq % tt == 0 and seq >= WINDOW and tt % WINDOW == 0
    tps = seq // tt
    n_tiles = bsz * tps

    def swa_tile(s):
        s = jnp.minimum(s, n_tiles - 1)
        return (s // tps, s % tps, 0)

    def mlp_tile(s):
        s = jnp.maximum(s - 2, 0)
        return (s // tps, s % tps, 0)

    kv_spec = pl.BlockSpec((None, WINDOW, KV_DIM), lambda s: (jnp.minimum(s, n_tiles - 1) // tps, 0, 0))
    return pl.pallas_call(
        functools.partial(_layer1_prompt_kernel, tiles_per_seq=tps),
        grid=(n_tiles + 2,),
        in_specs=[pl.BlockSpec(memory_space=pltpu.SMEM), pl.BlockSpec((None, tt, D_MODEL), swa_tile)]
        + _swa_weight_specs() + _mlp_weight_specs(),
        out_specs=[pl.BlockSpec((None, tt, D_MODEL), mlp_tile), kv_spec, kv_spec],
        out_shape=[jax.ShapeDtypeStruct((bsz, seq, D_MODEL), F32),
                   jax.ShapeDtypeStruct((bsz, WINDOW, KV_DIM), F32),
                   jax.ShapeDtypeStruct((bsz, WINDOW, KV_DIM), F32)],
        scratch_shapes=[pltpu.VMEM((tt, Q_DIM), BF16),
                        pltpu.VMEM((tt + WINDOW, KV_DIM), BF16),
                        pltpu.VMEM((tt + WINDOW, KV_DIM), BF16),
                        pltpu.VMEM((KV_DIM, tt + WINDOW), BF16),
                        pltpu.VMEM((tt, Q_DIM), BF16),
                        pltpu.VMEM((WINDOW, KV_DIM), F32),
                        pltpu.VMEM((WINDOW, KV_DIM), F32),
                        pltpu.VMEM((2, tt, D_MODEL), F32),
                        pltpu.VMEM((tt, D_MODEL), F32),
                        pltpu.VMEM((tt, D_MODEL), F32)],
        compiler_params=pltpu.CompilerParams(
            dimension_semantics=("arbitrary",), vmem_limit_bytes=VMEM_LIMIT),
        name="layer1_prompt",
    )(sinks, x, *swa_weights, *mlp_weights)


def _swa_sample_kernel(sink_ref, x_ref, ck_ref, cv_ref, wqkv_ref, bqkv_ref, wo_ref, bo_ref, g_ref, b_ref,
                       y_ref, ks_ref, vs_ref,
                       q_scr, knew_scr, vnew_scr, o_scr, *, nt):
    pt = SMP_PAD_T
    wbuf = ck_ref.shape[1]
    nkeys = 2 * WINDOW
    step = pl.program_id(0)

    @pl.when(step == 0)
    def _():
        qkv = _dot(x_ref[...].astype(BF16), wqkv_ref[...]) + bqkv_ref[...]
        q_scr[...] = qkv[:, :Q_DIM] * (HEAD_DIM ** -0.5)
        knew_scr[...] = qkv[:, Q_DIM:Q_DIM + KV_DIM]
        vnew_scr[...] = qkv[:, Q_DIM + KV_DIM:]

    rows = GROUP * pt
    ri = lax.broadcasted_iota(jnp.int32, (rows, nkeys), 0)
    kj = lax.broadcasted_iota(jnp.int32, (rows, nkeys), 1)
    tq = ri % pt
    in_buf = kj < wbuf
    mask = (in_buf & (kj - wbuf + WINDOW > tq)) | (
        jnp.logical_not(in_buf) & (kj - wbuf <= tq) & (kj - wbuf < nt))
    gi = lax.broadcasted_iota(jnp.int32, (rows, 1), 0) // pt
    sink_cols = []
    for kh in range(N_KV):
        col = jnp.zeros((rows, 1), F32)
        for g in range(GROUP):
            col = jnp.where(gi == g, sink_ref[kh * GROUP + g], col)
        sink_cols.append(col)
    kv0_k = _lane_is_kv0((nkeys, KV_DIM))
    kv0_o = _lane_is_kv0((rows, KV_DIM))

    pad_rows = jnp.zeros((nkeys - wbuf - pt, KV_DIM), F32)
    tail_is_old = lax.broadcasted_iota(jnp.int32, (pt, KV_DIM), 0) < pt - nt

    def per_group(gi, carry):
        bs = [gi * SMP_GROUP + i for i in range(SMP_GROUP)]
        staged = []
        for b in bs:
            r0 = pl.multiple_of((step * SMP_BB + b) * pt, pt)
            k_new = knew_scr[pl.ds(r0, pt), :]
            v_new = vnew_scr[pl.ds(r0, pt), :]
            kb = jnp.concatenate([ck_ref[b], k_new, pad_rows], axis=0)
            vb = jnp.concatenate([cv_ref[b], v_new, pad_rows], axis=0).astype(BF16)
            qs = jnp.concatenate(
                [q_scr[pl.ds(r0, pt), g * KV_DIM:(g + 1) * KV_DIM] for g in range(GROUP)], axis=0)
            qs = qs.astype(BF16)
            scores = [_dot_nt(qs, jnp.where(kv0_k, kb, 0.0).astype(BF16)),
                      _dot_nt(qs, jnp.where(kv0_k, 0.0, kb).astype(BF16))]
            staged.append((b, r0, k_new, v_new, vb, scores))
        soft = [[_sink_softmax(sc[kh], mask, sink_cols[kh]) for kh in range(N_KV)]
                for (_, _, _, _, _, sc) in staged]
        for (b, r0, k_new, v_new, vb, _), pr in zip(staged, soft):
            o0 = _dot(pr[0][0], vb) * pr[0][1]
            o1 = _dot(pr[1][0], vb) * pr[1][1]
            o = jnp.where(kv0_o, o0, o1)
            for g in range(GROUP):
                o_scr[pl.ds(r0, pt), g * KV_DIM:(g + 1) * KV_DIM] = o[g * pt:(g + 1) * pt, :]
            for new, c_ref, out_ref in ((k_new, ck_ref, ks_ref), (v_new, cv_ref, vs_ref)):
                out_ref[b, 0:wbuf - pt, :] = c_ref[b, nt:wbuf - pt + nt, :]
                old_tail = pltpu.roll(c_ref[b, wbuf - pt:wbuf, :], pt - nt, axis=0)
                out_ref[b, wbuf - pt:wbuf, :] = jnp.where(
                    tail_is_old, old_tail, pltpu.roll(new, pt - nt, axis=0))
        return carry

    lax.fori_loop(0, SMP_BB // SMP_GROUP, per_group, 0)

    @pl.when(step == pl.num_programs(0) - 1)
    def _():
        out = _dot(o_scr[...].astype(BF16), wo_ref[...]) + bo_ref[...]
        y_ref[...] = _layer_norm(ALPHA * x_ref[...] + out, g_ref[...], b_ref[...])


def _swa_sample(x_pad, cache_k, cache_v, sinks, weights, nt):
    rows = x_pad.shape[0]
    nb, wbuf, _ = cache_k.shape
    assert nb % SMP_BB == 0 and rows == nb * SMP_PAD_T and nt <= SMP_PAD_T
    assert wbuf + SMP_PAD_T <= 2 * WINDOW and wbuf % SMP_PAD_T == 0
    cache_spec = pl.BlockSpec((SMP_BB, wbuf, KV_DIM), lambda i: (i, 0, 0))
    return pl.pallas_call(
        functools.partial(_swa_sample_kernel, nt=nt),
        grid=(nb // SMP_BB,),
        in_specs=[pl.BlockSpec(memory_space=pltpu.SMEM), _const_spec((rows, D_MODEL)),
                  cache_spec, cache_spec] + _swa_weight_specs(),
        out_specs=[_const_spec((rows, D_MODEL)), cache_spec, cache_spec],
        out_shape=[jax.ShapeDtypeStruct((rows, D_MODEL), F32),
                   jax.ShapeDtypeStruct((nb, wbuf, KV_DIM), F32),
                   jax.ShapeDtypeStruct((nb, wbuf, KV_DIM), F32)],
        scratch_shapes=[pltpu.VMEM((rows, Q_DIM), F32),
                        pltpu.VMEM((rows, KV_DIM), F32),
                        pltpu.VMEM((rows, KV_DIM), F32),
                        pltpu.VMEM((rows, Q_DIM), F32)],
        compiler_params=pltpu.CompilerParams(
            dimension_semantics=("arbitrary",), vmem_limit_bytes=VMEM_LIMIT),
        name="swa_sample",
    )(sinks, x_pad, cache_k, cache_v, *weights)


def _row(v):
    return v.reshape(1, -1).astype(F32)


def _pack_lru_weights(j, lru_w_x, lru_b_x, lru_w_y, lru_b_y, lru_conv_w, lru_conv_b,
                      lru_w_ga, lru_b_ga, lru_w_gi, lru_b_gi, lru_lam, lru_w_out, lru_b_out, g, b):
    def chunk_pack(ax, ay):
        lead = ax.shape[:-1]
        cw2 = 2 * LRU_BLOCK
        return jnp.concatenate([ax.reshape(*lead, -1, cw2), ay.reshape(*lead, -1, cw2)],
                               axis=-1).reshape(*lead, 2 * D_RNN)

    wxy = chunk_pack(lru_w_x[j].astype(BF16), lru_w_y[j].astype(BF16))
    bxy = chunk_pack(lru_b_x[j], lru_b_y[j]).reshape(1, -1)
    wg = jnp.concatenate([lru_w_ga[j], lru_w_gi[j]], axis=2).astype(BF16)
    bg = jnp.concatenate([lru_b_ga[j], lru_b_gi[j]], axis=1)[:, None, :]
    return (wxy, bxy, lru_conv_w[j], _row(lru_conv_b[j]), wg, bg, _row(lru_lam[j]),
            lru_w_out[j].astype(BF16), _row(lru_b_out[j]), _row(g), _row(b))


def _pack_swa_weights(j, swa_w_qkv, swa_b_qkv, swa_w_o, swa_b_o, g, b):
    w, bias = swa_w_qkv[j].astype(BF16), swa_b_qkv[j]
    wq = w[:, :Q_DIM].reshape(D_MODEL, N_KV, GROUP, HEAD_DIM).transpose(0, 2, 1, 3).reshape(D_MODEL, Q_DIM)
    bq = bias[:Q_DIM].reshape(N_KV, GROUP, HEAD_DIM).transpose(1, 0, 2).reshape(Q_DIM)
    wqkv = jnp.concatenate([wq, w[:, Q_DIM:]], axis=1)
    bqkv = jnp.concatenate([bq, bias[Q_DIM:]]).reshape(1, -1)
    wo = swa_w_o[j].astype(BF16).reshape(N_KV, GROUP, HEAD_DIM, D_MODEL).transpose(1, 0, 2, 3)
    return (wqkv, bqkv, wo.reshape(Q_DIM, D_MODEL), _row(swa_b_o[j]), _row(g), _row(b))


def kernel(x_prompt, x_sample, state_lru_h, state_lru_conv, cache_swa_k, cache_swa_v,
           lru_w_x, lru_b_x, lru_w_y, lru_b_y, lru_conv_w, lru_conv_b,
           lru_w_ga, lru_b_ga, lru_w_gi, lru_b_gi, lru_lam, lru_w_out, lru_b_out,
           swa_w_qkv, swa_b_qkv, swa_sinks, swa_w_o, swa_b_o,
           mlp_w_up, mlp_w_down, ln1_g, ln1_b, ln2_g, ln2_b):
    bsz, seq, _ = x_prompt.shape
    nb, nt, _ = x_sample.shape
    wbuf = cache_swa_k.shape[2]

    mlp_w = [(mlp_w_up[i].astype(BF16), mlp_w_down[i].astype(BF16), _row(ln2_g[i]), _row(ln2_b[i]))
             for i in range(DEPTH)]

    def mlp(x2d, i):
        return _mlp_block(x2d, *mlp_w[i])

    lw = _pack_lru_weights(0, lru_w_x, lru_b_x, lru_w_y, lru_b_y, lru_conv_w, lru_conv_b,
                           lru_w_ga, lru_b_ga, lru_w_gi, lru_b_gi, lru_lam, lru_w_out, lru_b_out,
                           ln1_g[0], ln1_b[0])
    wg = lw[4]
    zero = jnp.zeros_like(wg[0::2])
    wg_pairs = jnp.concatenate([jnp.concatenate([wg[0::2], zero], axis=2),
                                jnp.concatenate([zero, wg[1::2]], axis=2)], axis=1)
    lw_p = lw[:4] + (wg_pairs,) + lw[5:]
    xp, h_p, conv_p = _layer0_prompt(x_prompt, lw_p, mlp_w[0])

    xs_tm = x_sample.transpose(1, 0, 2).reshape(nt * nb, D_MODEL)
    cs_tm = state_lru_conv[0].transpose(1, 0, 2).reshape((CONV_W - 1) * nb, D_RNN)
    xs_tm, h_s, conv_s_tm = _lru_sample(xs_tm, state_lru_h[0], cs_tm, lw, nb, nt)
    xs_tm = mlp(xs_tm, 0)
    conv_s = conv_s_tm.reshape(CONV_W - 1, nb, D_RNN).transpose(1, 0, 2)

    sw = _pack_swa_weights(0, swa_w_qkv, swa_b_qkv, swa_w_o, swa_b_o, ln1_g[1], ln1_b[1])
    sinks = swa_sinks[0].astype(F32)
    xp, k_p, v_p = _layer1_prompt(xp, sinks, sw, mlp_w[1])

    xs_bm = xs_tm.reshape(nt, nb, D_MODEL).transpose(1, 0, 2)
    xs_pad = jnp.pad(xs_bm, ((0, 0), (0, SMP_PAD_T - nt), (0, 0))).reshape(nb * SMP_PAD_T, D_MODEL)
    ck = cache_swa_k[0].reshape(nb, wbuf, KV_DIM)
    cv = cache_swa_v[0].reshape(nb, wbuf, KV_DIM)
    ys_pad, k_s, v_s = _swa_sample(xs_pad, ck, cv, sinks, sw, nt)
    xs = ys_pad.reshape(nb, SMP_PAD_T, D_MODEL)[:, :nt].reshape(nb * nt, D_MODEL)
    xs = mlp(xs, 1).reshape(nb, nt, D_MODEL)

    kv_shape = (1, -1, min(WINDOW, seq), N_KV, HEAD_DIM)
    return (xp, xs,
            h_p.reshape(1, bsz, D_RNN), conv_p[None],
            k_p.reshape(kv_shape), v_p.reshape(kv_shape),
            h_s[None], conv_s[None],
            k_s.reshape(1, nb, wbuf, N_KV, HEAD_DIM), v_s.reshape(1, nb, wbuf, N_KV, HEAD_DIM))
```

```python
import functools
import inspect

import jax
import jax.numpy as jnp
from jax import lax
from jax.experimental import pallas as pl
from jax.experimental.pallas import tpu as pltpu

F32 = jnp.float32
BF16 = jnp.bfloat16

D_MODEL = 1024
D_RNN = 1024
D_FF = 4096
LRU_BLOCKS = 8
LRU_BLOCK = 128
CONV_W = 4
LRU_C = 8.0
N_HEADS = 16
HEAD_DIM = 64
N_KV = 2
GROUP = 8
WINDOW = 128
KV_DIM = N_KV * HEAD_DIM
Q_DIM = N_HEADS * HEAD_DIM
QKV_DIM = Q_DIM + 2 * KV_DIM
DEPTH = 2
ALPHA = (2.0 * DEPTH) ** 0.25
Q_SCALE = HEAD_DIM ** -0.5
LN_EPS = 1e-5

SUBLANES = 8
VMEM_LIMIT = 60 * 1024 * 1024

MLP_TM = 512
MLP_FC = 1024
LRU_TT = 512
LRU_IL = 256
SWA_TT = 512
SWA_PAIR = 2
L1_MLP_CHUNKS = 8
SMP_PAD_T = 8
SMP_BB = 16
SMP_GROUP = 8


def _const_spec(shape):
    nd = len(shape)
    return pl.BlockSpec(shape, lambda *_: (0,) * nd, pipeline_mode=pl.Buffered(1))


def _dot(a, b):
    return jnp.dot(a, b, preferred_element_type=F32)


def _dot_nt(a, b):
    return lax.dot_general(a, b, (((1,), (1,)), ((), ())), preferred_element_type=F32)


def _layer_norm(z, g, b):
    mu = jnp.mean(z, axis=-1, keepdims=True)
    d = z - mu
    var = jnp.mean(d * d, axis=-1, keepdims=True)
    return d * lax.rsqrt(var + LN_EPS) * g + b


def _sigmoid(x):
    return 0.5 * jnp.tanh(0.5 * x) + 0.5


def _mlp_acc(x, wup_ref, wdn_ref):
    xb = x.astype(BF16)
    acc = jnp.zeros(x.shape, F32)
    for j in range(D_FF // MLP_FC):
        h = _dot(xb, wup_ref[:, j * MLP_FC:(j + 1) * MLP_FC])
        h = jnp.square(jnp.maximum(h, 0.0)).astype(BF16)
        acc = acc + _dot(h, wdn_ref[j * MLP_FC:(j + 1) * MLP_FC, :])
    return acc


def _mlp_ln(x, wup_ref, wdn_ref, g_ref, b_ref):
    return _layer_norm(ALPHA * x + _mlp_acc(x, wup_ref, wdn_ref), g_ref[...], b_ref[...])


def _mlp_kernel(x_ref, wup_ref, wdn_ref, g_ref, b_ref, o_ref):
    o_ref[...] = _mlp_ln(x_ref[...], wup_ref, wdn_ref, g_ref, b_ref)


def _mlp_weight_specs():
    return [_const_spec((D_MODEL, D_FF)), _const_spec((D_FF, D_MODEL)),
            _const_spec((1, D_MODEL)), _const_spec((1, D_MODEL))]


def _mlp_block(x, w_up, w_dn, g, b):
    n = x.shape[0]
    tm = min(MLP_TM, n)
    assert n % tm == 0
    row_spec = pl.BlockSpec((tm, D_MODEL), lambda i: (i, 0))
    return pl.pallas_call(
        _mlp_kernel,
        grid=(n // tm,),
        in_specs=[row_spec] + _mlp_weight_specs(),
        out_specs=row_spec,
        out_shape=jax.ShapeDtypeStruct((n, D_MODEL), F32),
        compiler_params=pltpu.CompilerParams(
            dimension_semantics=("arbitrary",), vmem_limit_bytes=VMEM_LIMIT),
        name="mlp_ln",
    )(x, w_up, w_dn, g, b)


_LOG2_E = 1.4426950408889634
_GELU_C0 = (2.0 / jnp.pi) ** 0.5
_GELU_C1 = _GELU_C0 * 0.044715


def _gelu_tanh(x):
    t = jnp.tanh(x * (_GELU_C0 + _GELU_C1 * (x * x)))
    hx = 0.5 * x
    return hx + hx * t


def _lru_in_proj(x, wxy_ref, bxy_ref):
    uy = _dot(x.astype(BF16), wxy_ref[...]) + bxy_ref[...]
    cw2 = 2 * LRU_BLOCK
    n_col = D_RNN // cw2
    u = jnp.concatenate([uy[:, 2 * j * cw2:(2 * j + 1) * cw2] for j in range(n_col)], axis=1)
    y_pre = jnp.concatenate([uy[:, (2 * j + 1) * cw2:(2 * j + 2) * cw2] for j in range(n_col)], axis=1)
    return u, _gelu_tanh(y_pre)


def _neg_c_softplus_neg(lam):
    z = -lam
    sp = jnp.maximum(z, 0.0) + jnp.log1p(jnp.exp(-jnp.abs(z)))
    return -LRU_C * sp


def _lru_gate_chunk(xc_n, n, wg_ref, bg_ref, nla):
    return _lru_gate_math(_dot(xc_n.astype(BF16), wg_ref[n]), xc_n, n, bg_ref, nla)


def _lru_gate_math(gt, xc_n, n, bg_ref, nla, half_scaled=False):
    lo = n * LRU_BLOCK
    gt = gt + bg_ref[n]
    if half_scaled:
        neg_half_nla = -0.5 * nla[:, lo:lo + LRU_BLOCK]
        neg_log_a = jnp.tanh(gt[:, :LRU_BLOCK]) * neg_half_nla + neg_half_nla
        gi = 0.5 * jnp.tanh(gt[:, LRU_BLOCK:]) + 0.5
        a = jnp.exp2(neg_log_a * (-_LOG2_E))
        v = jnp.tanh(neg_log_a) * (a * a + 1.0)
    else:
        gi = _sigmoid(gt[:, LRU_BLOCK:])
        log_a = _sigmoid(gt[:, :LRU_BLOCK]) * nla[:, lo:lo + LRU_BLOCK]
        a = jnp.exp(log_a)
        v = -jnp.tanh(log_a) * (a * a + 1.0)
    mult = jnp.where(v > 0.0, v * lax.rsqrt(v), 0.0)
    return a, mult * (gi * xc_n)


def _lru_out(h, y, x, wo_ref, bo_ref, g_ref, b_ref):
    out = _dot((h * y).astype(BF16), wo_ref[...]) + bo_ref[...]
    return _layer_norm(ALPHA * x + out, g_ref[...], b_ref[...])


def _interleave_matrix(tt):
    r = jnp.arange(tt)
    src_time = (r % SUBLANES) * (tt // SUBLANES) + r // SUBLANES
    return (src_time[:, None] == jnp.arange(tt)[None, :]).astype(BF16)


def _groups(z):
    return [z[r:r + LRU_IL, :] for r in range(0, z.shape[0], LRU_IL)]


def _shift_one_step(z, carry_row):
    row = lax.broadcasted_iota(jnp.int32, (SUBLANES, z.shape[1]), 0)
    out = []
    for zg in _groups(z):
        last = zg[LRU_IL - SUBLANES:, :]
        out.append(jnp.where(row == 0, carry_row, pltpu.roll(last, 1, axis=0)))
        out.append(zg[:LRU_IL - SUBLANES, :])
        carry_row = zg[LRU_IL - 1:, :]
    return jnp.concatenate(out, axis=0)


def _scan_interleaved(a, b, h_in):
    parts = []
    for ag, bg in zip(_groups(a), _groups(b)):
        hg, h_in = _scan_group(ag, bg, h_in)
        parts.append(hg)
    return jnp.concatenate(parts, axis=0), h_in


def _scan_group(a, b, h_in):
    n_steps = a.shape[0] // SUBLANES
    h = b[0:SUBLANES, :]
    p = a[0:SUBLANES, :]
    hs, ps = [h], [p]
    for i in range(1, n_steps):
        ai = a[i * SUBLANES:(i + 1) * SUBLANES, :]
        h = ai * h + b[i * SUBLANES:(i + 1) * SUBLANES, :]
        p = ai * p
        hs.append(h)
        ps.append(p)
    row = lax.broadcasted_iota(jnp.int32, h.shape, 0)
    cin = h_in
    cvec = jnp.zeros(h.shape, F32)
    for c in range(SUBLANES):
        cvec = jnp.where(row == c, cin, cvec)
        cin = h[c:c + 1, :] + p[c:c + 1, :] * cin
    full = jnp.concatenate([hs[i] + ps[i] * cvec for i in range(n_steps)], axis=0)
    return full, cin


def _drain_steps(s, n_tiles, g1_ref, b1_ref, wup_ref, wdn_ref, g2_ref, b2_ref, y_ref, x1_scr, z1_scr, acc_scr):
    slot = s % 2

    @pl.when(s == n_tiles)
    def _():
        x1 = _layer_norm(z1_scr[...], g1_ref[...], b1_ref[...])
        x1_scr[slot] = x1
        y_ref[...] = _layer_norm(ALPHA * x1_scr[1 - slot] + acc_scr[...], g2_ref[...], b2_ref[...])
        acc_scr[...] = _mlp_acc(x1, wup_ref, wdn_ref)

    @pl.when(s == n_tiles + 1)
    def _():
        y_ref[...] = _layer_norm(ALPHA * x1_scr[1 - slot] + acc_scr[...], g2_ref[...], b2_ref[...])


_DRAIN_REFS = ("g1_ref", "b1_ref", "wup_ref", "wdn_ref", "g2_ref", "b2_ref", "y_ref", "x1_scr", "z1_scr", "acc_scr")


def _fused_layer_kernel(tile_step, *refs, tiles_per_seq):
    named = inspect.signature(tile_step).bind(*refs, tiles_per_seq=tiles_per_seq).arguments
    s = pl.program_id(0)
    n_tiles = pl.num_programs(0) - 2

    @pl.when(s < n_tiles)
    def _():
        tile_step(*refs, tiles_per_seq=tiles_per_seq)

    _drain_steps(s, n_tiles, *(named[k] for k in _DRAIN_REFS))


def _layer0_tile_step(x_ref, pm_ref, pmt_ref, wxy_ref, bxy_ref, cw_ref, cb_ref, wg_ref, bg_ref, lam_ref,
                      wo_ref, bo_ref, g1_ref, b1_ref, wup_ref, wdn_ref, g2_ref, b2_ref,
                      y_ref, hout_ref, cout_ref,
                      hy_scr, hc_scr, cc_scr, ul_scr, x1_scr, z1_scr, acc_scr, *, tiles_per_seq):
    tt = LRU_TT
    s = pl.program_id(0)
    n_tiles = pl.num_programs(0) - 2
    t = s % tiles_per_seq
    slot = s % 2

    @pl.when(s == 0)
    def _():
        x1_scr[1] = jnp.zeros((tt, D_MODEL), F32)
        z1_scr[...] = jnp.zeros((tt, D_MODEL), F32)
        acc_scr[...] = jnp.zeros((tt, D_MODEL), F32)

    @pl.when(t == 0)
    def _():
        hc_scr[...] = jnp.zeros(hc_scr.shape, F32)
        cc_scr[...] = jnp.zeros(cc_scr.shape, F32)

    x = x_ref[...]
    xb = x.astype(BF16)
    acc = None
    nla = _neg_c_softplus_neg(lam_ref[...])
    n_mlp = LRU_BLOCKS
    fc = D_FF // n_mlp
    cw2 = 2 * LRU_BLOCK
    n_col = D_RNN // cw2

    def mlp_up(n):
        h = _dot(x1b, wup_ref[:, n * fc:(n + 1) * fc])
        return jnp.square(jnp.maximum(h, 0.0)).astype(BF16)

    def mlp_down(acc, h, n):
        d = _dot(h, wdn_ref[n * fc:(n + 1) * fc, :])
        return d if acc is None else acc + d

    def in_proj(j):
        return _dot(xpb, wxy_ref[:, 2 * j * cw2:2 * (j + 1) * cw2]) + bxy_ref[:, 2 * j * cw2:2 * (j + 1) * cw2]

    xp = jnp.concatenate([_dot(pm_ref[...], g) for g in _groups(xb)], axis=0)
    xpb = xp.astype(BF16)
    uy = in_proj(0)
    x1 = _layer_norm(z1_scr[...], g1_ref[...], b1_ref[...])
    x1_scr[slot] = x1
    x1b = x1.astype(BF16)
    hs = {}

    def mlp_ops(acc, ops):
        for kind, n in ops:
            if kind == "u":
                hs[n] = mlp_up(n)
            else:
                acc = mlp_down(acc, hs.pop(n), n)
        return acc

    phase_ops = [[("u", 1), ("d", 0), ("u", 2)], [("d", 1), ("u", 3), ("d", 2)],
                 [("u", 4), ("d", 3), ("u", 5)], [("d", 4), ("u", 6), ("d", 5)]]
    acc = mlp_ops(acc, [("u", 0)])
    for j in range(n_col):
        c0 = j * cw2
        acc = mlp_ops(acc, phase_ops[j])
        if j == 0:
            y_ref[...] = _layer_norm(ALPHA * x1_scr[1 - slot] + acc_scr[...], g2_ref[...], b2_ref[...])
        u = uy[:, :cw2]
        y = _gelu_tanh(uy[:, cw2:])
        z = cw_ref[0:1, c0:c0 + cw2] * u
        for k in range(1, CONV_W):
            carry = cc_scr[k - 1:k, c0:c0 + cw2]
            cc_scr[k - 1:k, c0:c0 + cw2] = z[tt - 1:tt, :]
            z = cw_ref[k:k + 1, c0:c0 + cw2] * u + _shift_one_step(z, carry)
        xc = z + cb_ref[:, c0:c0 + cw2]
        ul_scr[:, c0:c0 + cw2] = u[tt - (CONV_W - 1) * SUBLANES:, :]
        xcs = [xc[:, q * LRU_BLOCK:(q + 1) * LRU_BLOCK] for q in range(2)]
        gt2 = _dot(xc.astype(BF16), wg_ref[j])
        gts = [gt2[:, q * cw2:(q + 1) * cw2] for q in range(2)]
        if j + 1 < n_col:
            uy = in_proj(j + 1)
        for q in range(2):
            n = 2 * j + q
            lo = n * LRU_BLOCK
            a, b = _lru_gate_math(gts[q], xcs[q], n, bg_ref, nla, half_scaled=True)
            h, h_end = _scan_interleaved(a, b, hc_scr[0:1, lo:lo + LRU_BLOCK])
            hc_scr[0:1, lo:lo + LRU_BLOCK] = h_end
            hy_scr[:, lo:lo + LRU_BLOCK] = (h * y[:, q * LRU_BLOCK:(q + 1) * LRU_BLOCK]).astype(BF16)

    acc = mlp_ops(acc, [("u", 7), ("d", 6)])
    hy = jnp.concatenate([_dot(pmt_ref[...], g) for g in _groups(hy_scr[...])], axis=0)
    acc = mlp_ops(acc, [("d", 7)])
    out = _dot(hy.astype(BF16), wo_ref[...]) + bo_ref[...]
    assert not hs and n_mlp == 8 and n_col == len(phase_ops)
    z1_scr[...] = ALPHA * x + out
    acc_scr[...] = acc

    @pl.when(jnp.logical_and(t == tiles_per_seq - 1, s < n_tiles))
    def _():
        hout_ref[...] = hc_scr[0:1, :]
        for k in range(CONV_W - 1):
            r = (k + 1) * SUBLANES - 1
            cout_ref[k:k + 1, :] = ul_scr[r:r + 1, :]


def _lru_sample_kernel(x_ref, h0_ref, cs_ref, wxy_ref, bxy_ref, cw_ref, cb_ref, wg_ref, bg_ref,
                       lam_ref, wo_ref, bo_ref, g_ref, b_ref,
                       y_ref, hout_ref, cout_ref, *, nb, nt):
    x = x_ref[...]
    u, y = _lru_in_proj(x, wxy_ref, bxy_ref)
    upad = jnp.concatenate([cs_ref[...], u], axis=0)
    xc = cb_ref[...]
    for k in range(CONV_W):
        xc = xc + cw_ref[k:k + 1, :] * upad[k * nb:(k + nt) * nb, :]
    nla = _neg_c_softplus_neg(lam_ref[...])
    a_parts, b_parts = [], []
    for n in range(LRU_BLOCKS):
        lo = n * LRU_BLOCK
        a, b = _lru_gate_chunk(xc[:, lo:lo + LRU_BLOCK], n, wg_ref, bg_ref, nla)
        a_parts.append(a)
        b_parts.append(b)
    a = jnp.concatenate(a_parts, axis=1)
    b = jnp.concatenate(b_parts, axis=1)
    h = h0_ref[...]
    hs = []
    for t in range(nt):
        h = a[t * nb:(t + 1) * nb, :] * h + b[t * nb:(t + 1) * nb, :]
        hs.append(h)
    y_ref[...] = _lru_out(jnp.concatenate(hs, axis=0), y, x, wo_ref, bo_ref, g_ref, b_ref)
    hout_ref[...] = h
    cout_ref[...] = upad[nt * nb:, :]


def _lru_weight_specs(paired_gates=False):
    pair = 2 if paired_gates else 1
    return [_const_spec((D_MODEL, 2 * D_RNN)), _const_spec((1, 2 * D_RNN)),
            _const_spec((CONV_W, D_RNN)), _const_spec((1, D_RNN)),
            _const_spec((LRU_BLOCKS // pair, pair * LRU_BLOCK, pair * 2 * LRU_BLOCK)),
            _const_spec((LRU_BLOCKS, 1, 2 * LRU_BLOCK)),
            _const_spec((1, D_RNN)), _const_spec((D_RNN, D_MODEL)), _const_spec((1, D_MODEL)),
            _const_spec((1, D_MODEL)), _const_spec((1, D_MODEL))]


def _layer0_prompt(x, lru_weights, mlp_weights):
    bsz, seq, _ = x.shape
    tt = LRU_TT
    assert seq % tt == 0
    tps = seq // tt
    n_tiles = bsz * tps

    def lru_tile(s):
        s = jnp.minimum(s, n_tiles - 1)
        return (s // tps, s % tps, 0)

    def mlp_tile(s):
        s = jnp.maximum(s - 2, 0)
        return (s // tps, s % tps, 0)

    def state_block(s):
        return (jnp.minimum(s, n_tiles - 1) // tps, 0, 0)

    assert tt % LRU_IL == 0
    pm = _interleave_matrix(LRU_IL)
    return pl.pallas_call(
        functools.partial(_fused_layer_kernel, _layer0_tile_step, tiles_per_seq=tps),
        grid=(n_tiles + 2,),
        in_specs=[pl.BlockSpec((None, tt, D_MODEL), lru_tile), _const_spec((LRU_IL, LRU_IL)),
                  _const_spec((LRU_IL, LRU_IL))]
        + _lru_weight_specs(paired_gates=True) + _mlp_weight_specs(),
        out_specs=[pl.BlockSpec((None, tt, D_MODEL), mlp_tile),
                   pl.BlockSpec((None, 1, D_RNN), state_block),
                   pl.BlockSpec((None, CONV_W - 1, D_RNN), state_block)],
        out_shape=[jax.ShapeDtypeStruct((bsz, seq, D_MODEL), F32),
                   jax.ShapeDtypeStruct((bsz, 1, D_RNN), F32),
                   jax.ShapeDtypeStruct((bsz, CONV_W - 1, D_RNN), F32)],
        scratch_shapes=[pltpu.VMEM((tt, D_RNN), BF16),
                        pltpu.VMEM((SUBLANES, D_RNN), F32),
                        pltpu.VMEM((SUBLANES, D_RNN), F32),
                        pltpu.VMEM(((CONV_W - 1) * SUBLANES, D_RNN), F32),
                        pltpu.VMEM((2, tt, D_MODEL), F32),
                        pltpu.VMEM((tt, D_MODEL), F32),
                        pltpu.VMEM((tt, D_MODEL), F32)],
        compiler_params=pltpu.CompilerParams(
            dimension_semantics=("arbitrary",), vmem_limit_bytes=VMEM_LIMIT),
        name="layer0_prompt",
    )(x, pm, pm.T, *lru_weights, *mlp_weights)


def _lru_sample(x_tm, h0, cs_tm, weights, nb, nt):
    rows = nb * nt
    return pl.pallas_call(
        functools.partial(_lru_sample_kernel, nb=nb, nt=nt),
        grid=(1,),
        in_specs=[_const_spec((rows, D_MODEL)), _const_spec((nb, D_RNN)),
                  _const_spec(((CONV_W - 1) * nb, D_RNN))] + _lru_weight_specs(),
        out_specs=[_const_spec((rows, D_MODEL)), _const_spec((nb, D_RNN)),
                   _const_spec(((CONV_W - 1) * nb, D_RNN))],
        out_shape=[jax.ShapeDtypeStruct((rows, D_MODEL), F32),
                   jax.ShapeDtypeStruct((nb, D_RNN), F32),
                   jax.ShapeDtypeStruct(((CONV_W - 1) * nb, D_RNN), F32)],
        compiler_params=pltpu.CompilerParams(
            dimension_semantics=("arbitrary",), vmem_limit_bytes=VMEM_LIMIT),
        name="lru_sample",
    )(x_tm, h0, cs_tm, *weights)


def _lane_is_kv0(shape):
    return lax.broadcasted_iota(jnp.int32, shape, len(shape) - 1) < HEAD_DIM


def _sink_softmax(s, mask, sink):
    s = jnp.where(mask, s, -jnp.inf)
    m = jnp.maximum(jnp.max(s, axis=-1, keepdims=True), sink)
    p = jnp.exp(s - m)
    denom = jnp.sum(p, axis=-1, keepdims=True) + jnp.exp(sink - m)
    return p.astype(BF16), 1.0 / denom


def _sink_softmax_t(s2, mask, sink):
    sink2 = sink * _LOG2_E
    s2 = jnp.where(mask, s2, -jnp.inf)
    m = jnp.maximum(jnp.max(s2, axis=0, keepdims=True), sink2)
    p = jnp.exp2(s2 - m)
    denom = jnp.sum(p, axis=0, keepdims=True) + jnp.exp2(sink2 - m)
    return p.astype(BF16), 1.0 / denom


def _layer1_tile_step(sink_ref, x_ref, wqkv_ref, bqkv_ref, wo_ref, bo_ref, g1_ref, b1_ref,
                      wup_ref, wdn_ref, g2_ref, b2_ref,
                      y_ref, kout_ref, vout_ref,
                      q_scr, k0_scr, k1_scr, vt_scr, o_scr, kl_scr, vl_scr, x1_scr, z1_scr, acc_scr,
                      *, tiles_per_seq):
    tt = SWA_TT
    blk = WINDOW
    s_id = pl.program_id(0)
    n_tiles = pl.num_programs(0) - 2
    t = s_id % tiles_per_seq
    slot = s_id % 2

    @pl.when(s_id == 0)
    def _():
        x1_scr[1] = jnp.zeros((tt, D_MODEL), F32)
        z1_scr[...] = jnp.zeros((tt, D_MODEL), F32)
        acc_scr[...] = jnp.zeros((tt, D_MODEL), F32)

    @pl.when(t == 0)
    def _():
        zeros = jnp.zeros((blk, KV_DIM), BF16)
        k0_scr[0:blk, :] = zeros
        k1_scr[0:blk, :] = zeros
        vt_scr[:, 0:blk] = jnp.zeros((KV_DIM, blk), BF16)

    x = x_ref[...]
    xb = x.astype(BF16)
    acc = None
    n_mlp = L1_MLP_CHUNKS
    fc = D_FF // n_mlp
    hs = {}
    nxt_up = [0]
    nxt_dn = [0]

    def mlp_up():
        n = nxt_up[0]
        nxt_up[0] += 1
        h = _dot(x1b, wup_ref[:, n * fc:(n + 1) * fc])
        hs[n] = jnp.square(jnp.maximum(h, 0.0)).astype(BF16)

    def mlp_down(acc):
        n = nxt_dn[0]
        nxt_dn[0] += 1
        d = _dot(hs.pop(n), wdn_ref[n * fc:(n + 1) * fc, :])
        return d if acc is None else acc + d

    def mlp_fill(acc):
        if nxt_up[0] < n_mlp:
            mlp_up()
        if nxt_dn[0] < nxt_up[0] - 1 or (nxt_up[0] == n_mlp and nxt_dn[0] < n_mlp):
            acc = mlp_down(acc)
        return acc

    qkv =_dot(xb, wqkv_ref[...]) + bqkv_ref[...]
    x1 = _layer_norm(z1_scr[...], g1_ref[...], b1_ref[...])
    x1_scr[slot] = x1
    x1b = x1.astype(BF16)
    mlp_up()
    if n_mlp - 2 * (tt // blk) >= 4:
        acc = mlp_fill(acc)
    y_ref[...] = _layer_norm(ALPHA * x1_scr[1 - slot] + acc_scr[...], g2_ref[...], b2_ref[...])
    k = qkv[:, Q_DIM:Q_DIM + KV_DIM]
    v = qkv[:, Q_DIM + KV_DIM:]
    q_scr[...] = (qkv[:, :Q_DIM] * (Q_SCALE * _LOG2_E)).astype(BF16)
    kv0 = _lane_is_kv0((tt, KV_DIM))
    k0_scr[blk:, :] = jnp.where(kv0, k, 0.0).astype(BF16)
    k1_scr[blk:, :] = jnp.where(kv0, 0.0, k).astype(BF16)
    for i in range(tt // blk):
        vt_scr[:, (i + 1) * blk:(i + 2) * blk] = v[i * blk:(i + 1) * blk, :].T.astype(BF16)
    kl_scr[...] = k[tt - blk:, :]
    vl_scr[...] = v[tt - blk:, :]

    kj = lax.broadcasted_iota(jnp.int32, (2 * blk, blk), 0)
    qi = lax.broadcasted_iota(jnp.int32, (2 * blk, blk), 1)
    band = (kj > qi) & (kj <= qi + blk)
    kv0_rows = lax.broadcasted_iota(jnp.int32, (KV_DIM, GROUP * blk), 0) < HEAD_DIM

    n_blk = tt // blk
    def score_matmul(i):
        r0 = i * blk
        qs = jnp.concatenate(
            [q_scr[r0:r0 + blk, g * KV_DIM:(g + 1) * KV_DIM] for g in range(GROUP)], axis=0)
        kband = jnp.concatenate([k0_scr[r0:r0 + 2 * blk, :], k1_scr[r0:r0 + 2 * blk, :]], axis=0)
        return _dot_nt(kband, qs)

    def softmax_pv(i, scores_i):
        first = jnp.logical_and(t == 0, i == 0)
        mask = band & (kj >= jnp.where(first, blk, 0))
        ps, rs = [], []
        for kh in range(N_KV):
            s_kh = scores_i[kh * 2 * blk:(kh + 1) * 2 * blk, :]
            for g in range(GROUP):
                p, r = _sink_softmax_t(s_kh[:, g * blk:(g + 1) * blk], mask, sink_ref[kh * GROUP + g])
                ps.append(p)
                rs.append(r)
        ot = _dot(vt_scr[:, i * blk:(i + 2) * blk], jnp.concatenate(ps, axis=1))
        return ot, jnp.concatenate(rs, axis=1)

    def store_out(i, ot, r_all):
        r0 = i * blk
        ot = ot * r_all
        half = GROUP * blk
        ot = jnp.where(kv0_rows, ot[:, :half], ot[:, half:])
        for g in range(GROUP):
            o_scr[r0:r0 + blk, g * KV_DIM:(g + 1) * KV_DIM] = ot[:, g * blk:(g + 1) * blk].T.astype(BF16)

    for i0 in range(0, n_blk, SWA_PAIR):
        pair = range(i0, min(i0 + SWA_PAIR, n_blk))
        scores = {}
        for i in pair:
            scores[i] = score_matmul(i)
            acc = mlp_fill(acc)
        for i in pair:
            o, rs = softmax_pv(i, scores[i])
            acc = mlp_fill(acc)
            store_out(i, o, rs)

    out =_dot(o_scr[...], wo_ref[...]) + bo_ref[...]
    while nxt_dn[0] < n_mlp:
        acc = mlp_fill(acc)
    assert not hs and nxt_up[0] == n_mlp
    z1_scr[...] = ALPHA * x + out
    acc_scr[...] = acc

    k0_scr[0:blk, :] = k0_scr[tt:tt + blk, :]
    k1_scr[0:blk, :] = k1_scr[tt:tt + blk, :]
    vt_scr[:, 0:blk] = vt_scr[:, tt:tt + blk]

    @pl.when(jnp.logical_and(t == tiles_per_seq - 1, s_id < n_tiles))
    def _():
        kout_ref[...] = kl_scr[...]
        vout_ref[...] = vl_scr[...]


def _swa_weight_specs():
    return [_const_spec((D_MODEL, QKV_DIM)), _const_spec((1, QKV_DIM)),
            _const_spec((Q_DIM, D_MODEL)), _const_spec((1, D_MODEL)),
            _const_spec((1, D_MODEL)), _const_spec((1, D_MODEL))]


def _layer1_prompt(x, sinks, swa_weights, mlp_weights):
    bsz, seq, _ = x.shape
    tt = SWA_TT
    assert seq % tt == 0 and seq >= WINDOW and tt % WINDOW == 0
    tps = seq // tt
    n_tiles = bsz * tps

    def swa_tile(s):
        s = jnp.minimum(s, n_tiles - 1)
        return (s // tps, s % tps, 0)

    def mlp_tile(s):
        s = jnp.maximum(s - 2, 0)
        return (s // tps, s % tps, 0)

    kv_spec = pl.BlockSpec((None, WINDOW, KV_DIM), lambda s: (jnp.minimum(s, n_tiles - 1) // tps, 0, 0))
    return pl.pallas_call(
        functools.partial(_fused_layer_kernel, _layer1_tile_step, tiles_per_seq=tps),
        grid=(n_tiles + 2,),
        in_specs=[pl.BlockSpec(memory_space=pltpu.SMEM), pl.BlockSpec((None, tt, D_MODEL), swa_tile)]
        + _swa_weight_specs() + _mlp_weight_specs(),
        out_specs=[pl.BlockSpec((None, tt, D_MODEL), mlp_tile), kv_spec, kv_spec],
        out_shape=[jax.ShapeDtypeStruct((bsz, seq, D_MODEL), F32),
                   jax.ShapeDtypeStruct((bsz, WINDOW, KV_DIM), F32),
                   jax.ShapeDtypeStruct((bsz, WINDOW, KV_DIM), F32)],
        scratch_shapes=[pltpu.VMEM((tt, Q_DIM), BF16),
                        pltpu.VMEM((tt + WINDOW, KV_DIM), BF16),
                        pltpu.VMEM((tt + WINDOW, KV_DIM), BF16),
                        pltpu.VMEM((KV_DIM, tt + WINDOW), BF16),
                        pltpu.VMEM((tt, Q_DIM), BF16),
                        pltpu.VMEM((WINDOW, KV_DIM), F32),
                        pltpu.VMEM((WINDOW, KV_DIM), F32),
                        pltpu.VMEM((2, tt, D_MODEL), F32),
                        pltpu.VMEM((tt, D_MODEL), F32),
                        pltpu.VMEM((tt, D_MODEL), F32)],
        compiler_params=pltpu.CompilerParams(
            dimension_semantics=("arbitrary",), vmem_limit_bytes=VMEM_LIMIT),
        name="layer1_prompt",
    )(sinks, x, *swa_weights, *mlp_weights)


def _swa_sample_kernel(sink_ref, x_ref, ck_ref, cv_ref, wqkv_ref, bqkv_ref, wo_ref, bo_ref, g_ref, b_ref,
                       y_ref, ks_ref, vs_ref,
                       q_scr, knew_scr, vnew_scr, o_scr, *, nt):
    pt = SMP_PAD_T
    wbuf = ck_ref.shape[1]
    nkeys = 2 * WINDOW
    step = pl.program_id(0)

    @pl.when(step == 0)
    def _():
        qkv = _dot(x_ref[...].astype(BF16), wqkv_ref[...]) + bqkv_ref[...]
        q_scr[...] = qkv[:, :Q_DIM] * Q_SCALE
        knew_scr[...] = qkv[:, Q_DIM:Q_DIM + KV_DIM]
        vnew_scr[...] = qkv[:, Q_DIM + KV_DIM:]

    rows = GROUP * pt
    ri = lax.broadcasted_iota(jnp.int32, (rows, nkeys), 0)
    kj = lax.broadcasted_iota(jnp.int32, (rows, nkeys), 1)
    tq = ri % pt
    in_buf = kj < wbuf
    mask = (in_buf & (kj - wbuf + WINDOW > tq)) | (
        jnp.logical_not(in_buf) & (kj - wbuf <= tq) & (kj - wbuf < nt))
    gi = lax.broadcasted_iota(jnp.int32, (rows, 1), 0) // pt
    sink_cols = []
    for kh in range(N_KV):
        col = jnp.zeros((rows, 1), F32)
        for g in range(GROUP):
            col = jnp.where(gi == g, sink_ref[kh * GROUP + g], col)
        sink_cols.append(col)
    kv0_k = _lane_is_kv0((nkeys, KV_DIM))
    kv0_o = _lane_is_kv0((rows, KV_DIM))

    pad_rows = jnp.zeros((nkeys - wbuf - pt, KV_DIM), F32)
    tail_is_old = lax.broadcasted_iota(jnp.int32, (pt, KV_DIM), 0) < pt - nt

    def per_group(gi, carry):
        bs = [gi * SMP_GROUP + i for i in range(SMP_GROUP)]
        staged = []
        for b in bs:
            r0 = pl.multiple_of((step * SMP_BB + b) * pt, pt)
            k_new = knew_scr[pl.ds(r0, pt), :]
            v_new = vnew_scr[pl.ds(r0, pt), :]
            kb = jnp.concatenate([ck_ref[b], k_new, pad_rows], axis=0)
            vb = jnp.concatenate([cv_ref[b], v_new, pad_rows], axis=0).astype(BF16)
            qs = jnp.concatenate(
                [q_scr[pl.ds(r0, pt), g * KV_DIM:(g + 1) * KV_DIM] for g in range(GROUP)], axis=0)
            qs = qs.astype(BF16)
            scores = [_dot_nt(qs, jnp.where(kv0_k, kb, 0.0).astype(BF16)),
                      _dot_nt(qs, jnp.where(kv0_k, 0.0, kb).astype(BF16))]
            staged.append((b, r0, k_new, v_new, vb, scores))
        soft = [[_sink_softmax(sc[kh], mask, sink_cols[kh]) for kh in range(N_KV)]
                for (_, _, _, _, _, sc) in staged]
        for (b, r0, k_new, v_new, vb, _), pr in zip(staged, soft):
            o0 = _dot(pr[0][0], vb) * pr[0][1]
            o1 = _dot(pr[1][0], vb) * pr[1][1]
            o = jnp.where(kv0_o, o0, o1)
            for g in range(GROUP):
                o_scr[pl.ds(r0, pt), g * KV_DIM:(g + 1) * KV_DIM] = o[g * pt:(g + 1) * pt, :]
            for new, c_ref, out_ref in ((k_new, ck_ref, ks_ref), (v_new, cv_ref, vs_ref)):
                out_ref[b, 0:wbuf - pt, :] = c_ref[b, nt:wbuf - pt + nt, :]
                old_tail = pltpu.roll(c_ref[b, wbuf - pt:wbuf, :], pt - nt, axis=0)
                out_ref[b, wbuf - pt:wbuf, :] = jnp.where(
                    tail_is_old, old_tail, pltpu.roll(new, pt - nt, axis=0))
        return carry

    lax.fori_loop(0, SMP_BB // SMP_GROUP, per_group, 0)

    @pl.when(step == pl.num_programs(0) - 1)
    def _():
        out = _dot(o_scr[...].astype(BF16), wo_ref[...]) + bo_ref[...]
        y_ref[...] = _layer_norm(ALPHA * x_ref[...] + out, g_ref[...], b_ref[...])


def _swa_sample(x_pad, cache_k, cache_v, sinks, weights, nt):
    rows = x_pad.shape[0]
    nb, wbuf, _ = cache_k.shape
    assert nb % SMP_BB == 0 and rows == nb * SMP_PAD_T and nt <= SMP_PAD_T
    assert wbuf + SMP_PAD_T <= 2 * WINDOW and wbuf % SMP_PAD_T == 0
    cache_spec = pl.BlockSpec((SMP_BB, wbuf, KV_DIM), lambda i: (i, 0, 0))
    return pl.pallas_call(
        functools.partial(_swa_sample_kernel, nt=nt),
        grid=(nb // SMP_BB,),
        in_specs=[pl.BlockSpec(memory_space=pltpu.SMEM), _const_spec((rows, D_MODEL)),
                  cache_spec, cache_spec] + _swa_weight_specs(),
        out_specs=[_const_spec((rows, D_MODEL)), cache_spec, cache_spec],
        out_shape=[jax.ShapeDtypeStruct((rows, D_MODEL), F32),
                   jax.ShapeDtypeStruct((nb, wbuf, KV_DIM), F32),
                   jax.ShapeDtypeStruct((nb, wbuf, KV_DIM), F32)],
        scratch_shapes=[pltpu.VMEM((rows, Q_DIM), F32),
                        pltpu.VMEM((rows, KV_DIM), F32),
                        pltpu.VMEM((rows, KV_DIM), F32),
                        pltpu.VMEM((rows, Q_DIM), F32)],
        compiler_params=pltpu.CompilerParams(
            dimension_semantics=("arbitrary",), vmem_limit_bytes=VMEM_LIMIT),
        name="swa_sample",
    )(sinks, x_pad, cache_k, cache_v, *weights)


def _row(v):
    return v.reshape(1, -1).astype(F32)


def _pack_lru_weights(j, lru_w_x, lru_b_x, lru_w_y, lru_b_y, lru_conv_w, lru_conv_b,
                      lru_w_ga, lru_b_ga, lru_w_gi, lru_b_gi, lru_lam, lru_w_out, lru_b_out, g, b):
    def chunk_pack(ax, ay):
        lead = ax.shape[:-1]
        cw2 = 2 * LRU_BLOCK
        return jnp.concatenate([ax.reshape(*lead, -1, cw2), ay.reshape(*lead, -1, cw2)],
                               axis=-1).reshape(*lead, 2 * D_RNN)

    wxy = chunk_pack(lru_w_x[j].astype(BF16), lru_w_y[j].astype(BF16))
    bxy = chunk_pack(lru_b_x[j], lru_b_y[j]).reshape(1, -1)
    wg = jnp.concatenate([lru_w_ga[j], lru_w_gi[j]], axis=2).astype(BF16)
    bg = jnp.concatenate([lru_b_ga[j], lru_b_gi[j]], axis=1)[:, None, :]
    return (wxy, bxy, lru_conv_w[j], _row(lru_conv_b[j]), wg, bg, _row(lru_lam[j]),
            lru_w_out[j].astype(BF16), _row(lru_b_out[j]), _row(g), _row(b))


def _pack_swa_weights(j, swa_w_qkv, swa_b_qkv, swa_w_o, swa_b_o, g, b):
    w, bias = swa_w_qkv[j].astype(BF16), swa_b_qkv[j]
    wq = w[:, :Q_DIM].reshape(D_MODEL, N_KV, GROUP, HEAD_DIM).transpose(0, 2, 1, 3).reshape(D_MODEL, Q_DIM)
    bq = bias[:Q_DIM].reshape(N_KV, GROUP, HEAD_DIM).transpose(1, 0, 2).reshape(Q_DIM)
    wqkv = jnp.concatenate([wq, w[:, Q_DIM:]], axis=1)
    bqkv = jnp.concatenate([bq, bias[Q_DIM:]]).reshape(1, -1)
    wo = swa_w_o[j].astype(BF16).reshape(N_KV, GROUP, HEAD_DIM, D_MODEL).transpose(1, 0, 2, 3)
    return (wqkv, bqkv, wo.reshape(Q_DIM, D_MODEL), _row(swa_b_o[j]), _row(g), _row(b))


def kernel(x_prompt, x_sample, state_lru_h, state_lru_conv, cache_swa_k, cache_swa_v,
           lru_w_x, lru_b_x, lru_w_y, lru_b_y, lru_conv_w, lru_conv_b,
           lru_w_ga, lru_b_ga, lru_w_gi, lru_b_gi, lru_lam, lru_w_out, lru_b_out,
           swa_w_qkv, swa_b_qkv, swa_sinks, swa_w_o, swa_b_o,
           mlp_w_up, mlp_w_down, ln1_g, ln1_b, ln2_g, ln2_b):
    bsz, seq, _ = x_prompt.shape
    nb, nt, _ = x_sample.shape
    wbuf = cache_swa_k.shape[2]

    mlp_w = [(mlp_w_up[i].astype(BF16), mlp_w_down[i].astype(BF16), _row(ln2_g[i]), _row(ln2_b[i]))
             for i in range(DEPTH)]

    def mlp(x2d, i):
        return _mlp_block(x2d, *mlp_w[i])

    lw = _pack_lru_weights(0, lru_w_x, lru_b_x, lru_w_y, lru_b_y, lru_conv_w, lru_conv_b,
                           lru_w_ga, lru_b_ga, lru_w_gi, lru_b_gi, lru_lam, lru_w_out, lru_b_out,
                           ln1_g[0], ln1_b[0])
    wg = lw[4] * 0.5
    zero = jnp.zeros_like(wg[0::2])
    wg_pairs = jnp.concatenate([jnp.concatenate([wg[0::2], zero], axis=2),
                                jnp.concatenate([zero, wg[1::2]], axis=2)], axis=1)
    lw_p = lw[:4] + (wg_pairs, lw[5] * 0.5) + lw[6:]
    xp, h_p, conv_p = _layer0_prompt(x_prompt, lw_p, mlp_w[0])

    xs_tm = x_sample.transpose(1, 0, 2).reshape(nt * nb, D_MODEL)
    cs_tm = state_lru_conv[0].transpose(1, 0, 2).reshape((CONV_W - 1) * nb, D_RNN)
    xs_tm, h_s, conv_s_tm = _lru_sample(xs_tm, state_lru_h[0], cs_tm, lw, nb, nt)
    xs_tm = mlp(xs_tm, 0)
    conv_s = conv_s_tm.reshape(CONV_W - 1, nb, D_RNN).transpose(1, 0, 2)

    sw = _pack_swa_weights(0, swa_w_qkv, swa_b_qkv, swa_w_o, swa_b_o, ln1_g[1], ln1_b[1])
    sinks = swa_sinks[0].astype(F32)
    xp, k_p, v_p = _layer1_prompt(xp, sinks, sw, mlp_w[1])

    xs_bm = xs_tm.reshape(nt, nb, D_MODEL).transpose(1, 0, 2)
    xs_pad = jnp.pad(xs_bm, ((0, 0), (0, SMP_PAD_T - nt), (0, 0))).reshape(nb * SMP_PAD_T, D_MODEL)
    ck = cache_swa_k[0].reshape(nb, wbuf, KV_DIM)
    cv = cache_swa_v[0].reshape(nb, wbuf, KV_DIM)
    ys_pad, k_s, v_s = _swa_sample(xs_pad, ck, cv, sinks, sw, nt)
    xs = ys_pad.reshape(nb, SMP_PAD_T, D_MODEL)[:, :nt].reshape(nb * nt, D_MODEL)
    xs = mlp(xs, 1).reshape(nb, nt, D_MODEL)

    kv_shape = (1, -1, min(WINDOW, seq), N_KV, HEAD_DIM)
    return (xp, xs,
            h_p.reshape(1, bsz, D_RNN), conv_p[None],
            k_p.reshape(kv_shape), v_p.reshape(kv_shape),
            h_s[None], conv_s[None],
            k_s.reshape(1, nb, wbuf, N_KV, HEAD_DIM), v_s.reshape(1, nb, wbuf, N_KV, HEAD_DIM))
```
